```python
import math
import jax, jax.numpy as jnp
from jax import lax
import numpy as np

D_MODEL = 1024
BATCH = 8
SEQ = 4096
DEPTH = 1
DEC_BATCH = 32
DEC_SEQ = 4
PAST_LEN = 16384
PAGE_SIZE = 128

H_A = 4
DH_A = 64
DV_A = 2 * DH_A
H_B = 8
DH_B = 64
D_PLE = 256
Q_BLOCK = 128
EPS = 1e-6
NEG_INF = -1e30
W_QK_A = H_A * 2 * DH_A
W_V_A = H_A * DV_A
W_B = H_B * DH_B
MIX = W_V_A + W_B
D_IN = 2 * W_QK_A + W_V_A + 3 * W_B + MIX

kernel_name = 'hymba_diff_stickbreak_step'


def _rmsnorm(x, g):
    xf = x.astype(jnp.float32)
    y = xf * lax.rsqrt(jnp.mean(xf * xf, axis=-1, keepdims=True) + EPS)
    return (y * g.astype(jnp.float32)).astype(x.dtype)


def _split_proj(h, w_in):
    b, t = h.shape[0], h.shape[1]
    z = jnp.einsum('btd,de->bte', h, w_in)
    c1 = W_QK_A
    c2 = c1 + W_QK_A
    c3 = c2 + W_V_A
    c4 = c3 + W_B
    c5 = c4 + W_B
    c6 = c5 + W_B
    q_a = z[..., :c1].reshape(b, t, H_A, 2, DH_A)
    k_a = z[..., c1:c2].reshape(b, t, H_A, 2, DH_A)
    v_a = z[..., c2:c3].reshape(b, t, H_A, DV_A)
    q_b = z[..., c3:c4].reshape(b, t, H_B, DH_B)
    k_b = z[..., c4:c5].reshape(b, t, H_B, DH_B)
    v_b = z[..., c5:c6].reshape(b, t, H_B, DH_B)
    gate = z[..., c6:]
    return q_a, k_a, v_a, q_b, k_b, v_b, gate


def _alibi_slopes():
    return jnp.exp2(-8.0 * jnp.arange(1, H_A + 1, dtype=jnp.float32) / H_A)


def _diff_attention(q, k, v, q_pos, k_pos, lam):
    s = jnp.einsum('bqhcd,bkhcd->bhcqk', q, k, preferred_element_type=jnp.float32) * (DH_A ** -0.5)
    dist = (q_pos[:, None] - k_pos[None, :]).astype(jnp.float32)
    bias = -_alibi_slopes()[:, None, None] * dist
    s = jnp.where(dist >= 0, s + bias[None, :, None], NEG_INF)
    a = jax.nn.softmax(s, axis=-1)
    w = a[:, :, 0] - lam * a[:, :, 1]
    return jnp.einsum('bhqk,bkhd->bqhd', w.astype(v.dtype), v)


def _stick_breaking(q, k, v, q_pos, k_pos):
    z = jnp.einsum('bqhd,bkhd->bhqk', q, k, preferred_element_type=jnp.float32) * (DH_B ** -0.5)
    mask = k_pos[None, :] < q_pos[:, None]
    log_beta = jax.nn.log_sigmoid(z)
    log_rest = jnp.where(mask, jax.nn.log_sigmoid(-z), 0.0)
    after = lax.cumsum(log_rest, axis=3, reverse=True) - log_rest
    wts = jnp.where(mask, jnp.exp(log_beta + after), 0.0)
    return jnp.einsum('bhqk,bkhd->bqhd', wts.astype(v.dtype), v)


def _merge(o_a, o_b, gate, g_subln, lam_init, w_out):
    b, t = o_a.shape[0], o_a.shape[1]
    o_a = _rmsnorm(o_a, g_subln) * (1.0 - lam_init)
    o = jnp.concatenate([o_a.reshape(b, t, W_V_A), o_b.reshape(b, t, W_B)], axis=-1)
    return jnp.einsum('bte,ed->btd', o * jax.nn.silu(gate), w_out)


def _ple(x, p, g, w_gate, w_proj):
    gate = jax.nn.sigmoid(jnp.einsum('btd,de->bte', _rmsnorm(x, g), w_gate))
    return x + gate * jnp.einsum('btp,pd->btd', p, w_proj)


def _to_blocks(a, n_blk):
    return a.reshape(a.shape[0], n_blk, Q_BLOCK, *a.shape[2:]).swapaxes(0, 1)


def _from_blocks(a):
    return a.swapaxes(0, 1).reshape(a.shape[1], a.shape[0] * a.shape[2], *a.shape[3:])


def setup_inputs(seed: int = 0) -> dict:
    key = jax.random.key(seed)
    ks = jax.random.split(key, 24)
    f32 = jnp.float32
    n_pages = PAST_LEN // PAGE_SIZE
    n_used = DEC_BATCH * n_pages
    n_pool = n_used + max(1, n_used // 4)
    page_table = jax.random.permutation(ks[6], n_pool)[:n_used].reshape(DEC_BATCH, n_pages).astype(jnp.int32)
    return {
        'x_prompt': jax.random.normal(ks[0], (BATCH, SEQ, D_MODEL), f32),
        'x_sample': jax.random.normal(ks[1], (DEC_BATCH, DEC_SEQ, D_MODEL), f32),
        'cache_k_diff': jax.random.normal(ks[2], (DEPTH, n_pool, PAGE_SIZE, H_A, 2, DH_A), f32),
        'cache_v_diff': jax.random.normal(ks[3], (DEPTH, n_pool, PAGE_SIZE, H_A, DV_A), f32),
        'cache_k_sb': jax.random.normal(ks[4], (DEPTH, n_pool, PAGE_SIZE, H_B, DH_B), f32),
        'cache_v_sb': jax.random.normal(ks[5], (DEPTH, n_pool, PAGE_SIZE, H_B, DH_B), f32),
        'page_table': page_table,
        'p_prompt': jax.random.normal(ks[7], (DEPTH, BATCH, SEQ, D_PLE), f32),
        'p_sample': jax.random.normal(ks[8], (DEPTH, DEC_BATCH, DEC_SEQ, D_PLE), f32),
        'g_norm': 1.0 + 0.01 * jax.random.normal(ks[9], (DEPTH, D_MODEL), f32),
        'w_in': jax.random.normal(ks[10], (DEPTH, D_MODEL, D_IN), f32) * D_MODEL ** -0.5,
        'lambda_q1': 0.1 * jax.random.normal(ks[11], (DEPTH, DH_A), f32),
        'lambda_k1': 0.1 * jax.random.normal(ks[12], (DEPTH, DH_A), f32),
        'lambda_q2': 0.1 * jax.random.normal(ks[13], (DEPTH, DH_A), f32),
        'lambda_k2': 0.1 * jax.random.normal(ks[14], (DEPTH, DH_A), f32),
        'g_subln': 1.0 + 0.01 * jax.random.normal(ks[15], (DEPTH, DV_A), f32),
        'w_out': jax.random.normal(ks[16], (DEPTH, MIX, D_MODEL), f32) * MIX ** -0.5,
        'g_ple': 1.0 + 0.01 * jax.random.normal(ks[17], (DEPTH, D_MODEL), f32),
        'w_ple_gate': jax.random.normal(ks[18], (DEPTH, D_MODEL, D_MODEL), f32) * D_MODEL ** -0.5,
        'w_ple_proj': jax.random.normal(ks[19], (DEPTH, D_PLE, D_MODEL), f32) * D_PLE ** -0.5,
        'g_final': 1.0 + 0.01 * jax.random.normal(ks[20], (D_MODEL,), f32),
    }


def reference(x_prompt, x_sample, cache_k_diff, cache_v_diff, cache_k_sb, cache_v_sb, page_table,
              p_prompt, p_sample, g_norm, w_in, lambda_q1, lambda_k1, lambda_q2, lambda_k2,
              g_subln, w_out, g_ple, w_ple_gate, w_ple_proj, g_final):
    f32 = jnp.float32
    t = x_prompt.shape[1]
    db = x_sample.shape[0]
    ds = x_sample.shape[1]
    n_blk = t // Q_BLOCK
    past = page_table.shape[1] * PAGE_SIZE
    pos_p = jnp.arange(t, dtype=jnp.int32)
    k_pos_s = jnp.arange(past + ds, dtype=jnp.int32)
    q_pos_s = k_pos_s[past:]

    def gather(pool, l):
        rows = pool[l, page_table]
        return rows.reshape(db, past, *rows.shape[3:])

    xp, xs = x_prompt, x_sample
    nkd_p, nvd_p, nks_p, nvs_p = [], [], [], []
    nkd_s, nvd_s, nks_s, nvs_s = [], [], [], []
    for l in range(DEPTH):
        lam_init = 0.8 - 0.6 * math.exp(-0.3 * l)
        lam = (jnp.exp(jnp.sum(lambda_q1[l].astype(f32) * lambda_k1[l].astype(f32)))
               - jnp.exp(jnp.sum(lambda_q2[l].astype(f32) * lambda_k2[l].astype(f32))) + lam_init)

        h = _rmsnorm(xp, g_norm[l])
        q_a, k_a, v_a, q_b, k_b, v_b, gate = _split_proj(h, w_in[l])

        def block(args):
            qa_blk, qb_blk, qpos = args
            return (_diff_attention(qa_blk, k_a, v_a, qpos, pos_p, lam),
                    _stick_breaking(qb_blk, k_b, v_b, qpos, pos_p))

        o_a, o_b = lax.map(block, (_to_blocks(q_a, n_blk), _to_blocks(q_b, n_blk),
                                   pos_p.reshape(n_blk, Q_BLOCK)))
        xp = xp + _merge(_from_blocks(o_a), _from_blocks(o_b), gate, g_subln[l], lam_init, w_out[l])
        xp = _ple(xp, p_prompt[l], g_ple[l], w_ple_gate[l], w_ple_proj[l])
        nkd_p.append(k_a)
        nvd_p.append(v_a)
        nks_p.append(k_b)
        nvs_p.append(v_b)

        h = _rmsnorm(xs, g_norm[l])
        sq_a, sk_a, sv_a, sq_b, sk_b, sv_b, sgate = _split_proj(h, w_in[l])
        ka_all = jnp.concatenate([gather(cache_k_diff, l), sk_a], axis=1)
        va_all = jnp.concatenate([gather(cache_v_diff, l), sv_a], axis=1)
        kb_all = jnp.concatenate([gather(cache_k_sb, l), sk_b], axis=1)
        vb_all = jnp.concatenate([gather(cache_v_sb, l), sv_b], axis=1)
        so_a = _diff_attention(sq_a, ka_all, va_all, q_pos_s, k_pos_s, lam)
        so_b = _stick_breaking(sq_b, kb_all, vb_all, q_pos_s, k_pos_s)
        xs = xs + _merge(so_a, so_b, sgate, g_subln[l], lam_init, w_out[l])
        xs = _ple(xs, p_sample[l], g_ple[l], w_ple_gate[l], w_ple_proj[l])
        nkd_s.append(sk_a)
        nvd_s.append(sv_a)
        nks_s.append(sk_b)
        nvs_s.append(sv_b)

    y_prompt = _rmsnorm(xp, g_final)
    y_sample = _rmsnorm(xs, g_final)
    return (y_prompt, y_sample,
            jnp.stack(nkd_p), jnp.stack(nvd_p), jnp.stack(nks_p), jnp.stack(nvs_p),
            jnp.stack(nkd_s), jnp.stack(nvd_s), jnp.stack(nks_s), jnp.stack(nvs_s))
```

```python
import functools
import math

import jax
import jax.numpy as jnp
from jax import lax
from jax.experimental import pallas as pl
from jax.experimental.pallas import tpu as pltpu

F32 = jnp.float32
BF16 = jnp.bfloat16

H_A = 4
DH_A = 64
DV_A = 2 * DH_A
H_B = 8
DH_B = 64
EPS = 1e-6
NEG_INF = -1e30
W_GROUP = H_A * DV_A
N_GROUPS_QKV = 6
MIX = 2 * W_GROUP

LANES = 128
SUBLANES = 8
HALF = LANES // 2
VMEM_LIMIT = 56 * 1024 * 1024

ALIBI_LOG2_STEP = 8 // H_A
assert 8 % H_A == 0

TQ = 256
TK = 256
Q_PAD = SUBLANES
ROWS_S = 2 * H_A * Q_PAD
assert ROWS_S == H_B * Q_PAD


def _pow2_neg(e):
    return lax.bitcast_convert_type((127 - e) << 23, F32)


def _rms(x, g):
    return x * lax.rsqrt(jnp.mean(x * x, axis=-1, keepdims=True) + EPS) * g


def _nt_dot(a, b):
    return lax.dot_general(a, b, (((1,), (1,)), ((), ())), preferred_element_type=F32)


def _lambda(lam_ref, lam_init):
    t1 = jnp.sum(lam_ref[0] * lam_ref[1], axis=-1, keepdims=True)
    t2 = jnp.sum(lam_ref[2] * lam_ref[3], axis=-1, keepdims=True)
    return jnp.exp(t1) - jnp.exp(t2) + lam_init


def _softplus(z):
    return jnp.maximum(z, 0.0) + jnp.log(1.0 + jnp.exp(-jnp.abs(z)))


def _split_bf16(x):
    hi = x.astype(BF16)
    lo = (x - hi.astype(F32)).astype(BF16)
    return hi, lo


def _strict_lower_ones(n):
    r = lax.broadcasted_iota(jnp.int32, (n, n), 0)
    c = lax.broadcasted_iota(jnp.int32, (n, n), 1)
    return jnp.where(r > c, 1.0, 0.0).astype(BF16)


def _proj_kernel(x_ref, g_ref, w_ref, zb_ref, ka_ref, va_ref, kb_ref, vb_ref, gate_ref):
    h = _rms(x_ref[...], g_ref[...]).astype(BF16)

    def cols(i):
        return jnp.dot(h, w_ref[:, i * W_GROUP:(i + 1) * W_GROUP], preferred_element_type=F32)

    f32_outs = {1: ka_ref, 2: va_ref, 4: kb_ref, 5: vb_ref}
    for i in range(N_GROUPS_QKV):
        z = cols(i)
        if i in f32_outs:
            f32_outs[i][...] = z
        else:
            z = z * (DH_A ** -0.5)
        zb_ref[:, i * W_GROUP:(i + 1) * W_GROUP] = z.astype(BF16)
    for i in range(2):
        gate_ref[:, i * W_GROUP:(i + 1) * W_GROUP] = cols(N_GROUPS_QKV + i)


def _proj(x, g, w_bf16, tm):
    n, d = x.shape
    d_in = w_bf16.shape[1]
    assert n % tm == 0 and d_in == (N_GROUPS_QKV + 2) * W_GROUP
    row = lambda i: (i, 0)
    fixed = lambda i: (0, 0)
    kv = jax.ShapeDtypeStruct((n, W_GROUP), F32)
    return pl.pallas_call(
        _proj_kernel,
        grid=(n // tm,),
        in_specs=[pl.BlockSpec((tm, d), row), pl.BlockSpec((1, d), fixed),
                  pl.BlockSpec((d, d_in), fixed)],
        out_specs=[pl.BlockSpec((tm, N_GROUPS_QKV * W_GROUP), row)]
        + [pl.BlockSpec((tm, W_GROUP), row)] * 4 + [pl.BlockSpec((tm, MIX), row)],
        out_shape=[jax.ShapeDtypeStruct((n, N_GROUPS_QKV * W_GROUP), BF16), kv, kv, kv, kv,
                   jax.ShapeDtypeStruct((n, MIX), F32)],
        compiler_params=pltpu.CompilerParams(dimension_semantics=("parallel",),
                                             vmem_limit_bytes=VMEM_LIMIT),
        name="proj",
    )(x, g.reshape(1, d), w_bf16)


def _stack_halves(x):
    lane = lax.broadcasted_iota(jnp.int32, x.shape, 1)
    zero = jnp.zeros_like(x)
    return jnp.concatenate([jnp.where(lane < HALF, x, zero), jnp.where(lane >= HALF, x, zero)], axis=0)


def _diff_prompt_kernel(q_ref, k_ref, v_ref, lam_ref, g_ref, o_ref, *, lam_init):
    h = pl.program_id(1)
    qi = pl.program_id(2)
    qs = _stack_halves(q_ref[...])
    slope = _pow2_neg(jnp.full((1, TK), ALIBI_LOG2_STEP, jnp.int32) * (h + 1))
    col = lax.broadcasted_iota(jnp.int32, (1, TK), 1).astype(F32)

    def block(kj, carry, masked):
        m, l, acc = carry
        start = pl.multiple_of(kj * TK, TK)
        k = k_ref[pl.ds(start, TK), :]
        v = v_ref[pl.ds(start, TK), :]
        s = _nt_dot(qs, k)
        s = s + slope * (col + (kj * TK - qi * TQ).astype(F32))
        if masked:
            r = lax.broadcasted_iota(jnp.int32, (2 * TQ, TK), 0) & (TQ - 1)
            c = lax.broadcasted_iota(jnp.int32, (2 * TQ, TK), 1)
            s = jnp.where(c <= r, s, NEG_INF)
        m_new = jnp.maximum(m, jnp.max(s, axis=-1, keepdims=True))
        alpha = jnp.exp(m - m_new)
        p = jnp.exp(s - m_new)
        l = alpha * l + jnp.sum(p, axis=-1, keepdims=True)
        acc = alpha * acc + jnp.dot(p.astype(BF16), v, preferred_element_type=F32)
        return m_new, l, acc

    init = (jnp.full((2 * TQ, 1), NEG_INF, F32), jnp.zeros((2 * TQ, 1), F32),
            jnp.zeros((2 * TQ, LANES), F32))
    carry = lax.fori_loop(0, qi, lambda kj, c: block(kj, c, False), init)
    _, l, acc = block(qi, carry, True)
    o = acc / l
    d = o[:TQ] - _lambda(lam_ref, lam_init) * o[TQ:]
    o_ref[...] = _rms(d, g_ref[...]) * (1.0 - lam_init)


def _sb_prompt_kernel(q_ref, k_ref, v_ref, o_ref):
    qi = pl.program_id(2)
    qs = _stack_halves(q_ref[...])
    tri = _strict_lower_ones(TK)

    def block(kj, carry, masked):
        run, acc = carry
        start = pl.multiple_of(kj * TK, TK)
        k = k_ref[pl.ds(start, TK), :]
        vs = _stack_halves(v_ref[pl.ds(start, TK), :])
        z = _nt_dot(qs, k)
        sp = _softplus(z)
        log_rest = -sp
        if masked:
            r = lax.broadcasted_iota(jnp.int32, (2 * TQ, TK), 0) & (TQ - 1)
            c = lax.broadcasted_iota(jnp.int32, (2 * TQ, TK), 1)
            mask = c < r
            log_rest = jnp.where(mask, log_rest, 0.0)
        hi, lo = _split_bf16(log_rest)
        after = (jnp.dot(hi, tri, preferred_element_type=F32)
                 + jnp.dot(lo, tri, preferred_element_type=F32))
        w = jnp.exp((z - sp) + after + run)
        if masked:
            w = jnp.where(mask, w, 0.0)
        run = run + jnp.sum(log_rest, axis=-1, keepdims=True)
        wb = w.astype(BF16)
        acc = acc + jnp.dot(jnp.concatenate([wb[:TQ], wb[TQ:]], axis=1), vs,
                            preferred_element_type=F32)
        return run, acc

    init = (jnp.zeros((2 * TQ, 1), F32), jnp.zeros((TQ, LANES), F32))
    carry = block(qi, init, True)
    _, acc = lax.fori_loop(0, qi, lambda i, c: block(qi - 1 - i, c, False), carry)
    o_ref[...] = acc


def _prompt_attention(body, zb, extra, q_blk, k_blk, v_blk, name):
    b, t, _ = zb.shape
    assert t % TQ == 0 and TQ == TK
    n_grp = W_GROUP // LANES
    extra_specs = [pl.BlockSpec(e.shape, lambda bi, g, qi, nd=e.ndim: (0,) * nd) for e in extra]
    return pl.pallas_call(
        body,
        grid=(b, n_grp, t // TQ),
        in_specs=[pl.BlockSpec((None, TQ, LANES), lambda bi, g, qi: (bi, qi, q_blk * n_grp + g)),
                  pl.BlockSpec((None, t, LANES), lambda bi, g, qi: (bi, 0, k_blk * n_grp + g)),
                  pl.BlockSpec((None, t, LANES), lambda bi, g, qi: (bi, 0, v_blk * n_grp + g))]
        + extra_specs,
        out_specs=pl.BlockSpec((None, TQ, LANES), lambda bi, g, qi: (bi, qi, g)),
        out_shape=jax.ShapeDtypeStruct((b, t, W_GROUP), F32),
        compiler_params=pltpu.CompilerParams(
            dimension_semantics=("parallel", "parallel", "parallel"),
            vmem_limit_bytes=VMEM_LIMIT),
        name=name,
    )(zb, zb, zb, *extra)


def _sample_kernel(pt_ref, qa_ref, qb_ref, kna_ref, vna_ref, knb_ref, vnb_ref,
                   cka_ref, cva_ref, ckb_ref, cvb_ref, lam_ref, g_ref,
                   oa_ref, ob_ref, m_sc, l_sc, acca_sc, run_sc, accb_sc,
                   *, past, page, lam_init):
    del pt_ref
    j = pl.program_id(1)
    n_pages = past // page
    rows_h = 2 * Q_PAD
    r = lax.broadcasted_iota(jnp.int32, (ROWS_S, page), 0)
    c = lax.broadcasted_iota(jnp.int32, (ROWS_S, page), 1)
    q_of_row = r & (Q_PAD - 1)
    slope = _pow2_neg(ALIBI_LOG2_STEP * ((r // rows_h) + 1))
    tri = _strict_lower_ones(page)

    def attend(kta_ref, va_ref, ktb_ref, vtb_ref, base, masked):
        s = jnp.dot(qa_ref[...], kta_ref[...].astype(BF16), preferred_element_type=F32)
        s = s + slope * (c.astype(F32) + base)
        if masked:
            s = jnp.where(c <= q_of_row, s, NEG_INF)
        m = m_sc[...]
        m_new = jnp.maximum(m, jnp.max(s, axis=-1, keepdims=True))
        alpha = jnp.exp(m - m_new)
        p = jnp.exp(s - m_new)
        m_sc[...] = m_new
        l_sc[...] = alpha * l_sc[...] + jnp.sum(p, axis=-1, keepdims=True)
        p = p.astype(BF16)
        for h in range(H_A):
            rows = slice(h * rows_h, (h + 1) * rows_h)
            v_h = va_ref[pl.ds(h, page, stride=H_A), :].astype(BF16)
            acca_sc[rows, :] = (alpha[rows] * acca_sc[rows, :]
                                + jnp.dot(p[rows], v_h, preferred_element_type=F32))
        z = jnp.dot(qb_ref[...], ktb_ref[...].astype(BF16), preferred_element_type=F32)
        sp = _softplus(z)
        log_rest = -sp
        if masked:
            mask = c < q_of_row
            log_rest = jnp.where(mask, log_rest, 0.0)
        hi, lo = _split_bf16(log_rest)
        after = (jnp.dot(hi, tri, preferred_element_type=F32)
                 + jnp.dot(lo, tri, preferred_element_type=F32))
        w = jnp.exp((z - sp) + after + run_sc[...])
        if masked:
            w = jnp.where(mask, w, 0.0)
        run_sc[...] = run_sc[...] + jnp.sum(log_rest, axis=-1, keepdims=True)
        accb_sc[...] = accb_sc[...] + _nt_dot(w.astype(BF16), vtb_ref[...].astype(BF16))

    @pl.when(j == 0)
    def _():
        m_sc[...] = jnp.full(m_sc.shape, NEG_INF, F32)
        l_sc[...] = jnp.zeros(l_sc.shape, F32)
        acca_sc[...] = jnp.zeros(acca_sc.shape, F32)
        run_sc[...] = jnp.zeros(run_sc.shape, F32)
        accb_sc[...] = jnp.zeros(accb_sc.shape, F32)
        attend(kna_ref, vna_ref, knb_ref, vnb_ref, 0.0, True)

    @pl.when(j > 0)
    def _():
        base = ((n_pages - j) * page - past).astype(F32)
        attend(cka_ref, cva_ref, ckb_ref, cvb_ref, base, False)

    @pl.when(j == n_pages)
    def _():
        o = acca_sc[...] / l_sc[...]
        lam = _lambda(lam_ref, lam_init)
        for h in range(H_A):
            d = (o[h * rows_h:h * rows_h + Q_PAD]
                 - lam * o[h * rows_h + Q_PAD:(h + 1) * rows_h])
            oa_ref[:, h * DV_A:(h + 1) * DV_A] = _rms(d, g_ref[...]) * (1.0 - lam_init)
        acc = accb_sc[...]
        lane_head = lax.broadcasted_iota(jnp.int32, (Q_PAD, W_GROUP), 1) // DH_B
        out = jnp.zeros((Q_PAD, W_GROUP), F32)
        for h in range(H_B):
            out = out + jnp.where(lane_head == h, acc[h * Q_PAD:(h + 1) * Q_PAD], 0.0)
        ob_ref[...] = out


def _sample_attention(page_table, qbd_a, qbd_b, kn_a, vn_a, kn_b, vn_b,
                      ck_a, cv_a, ck_b, cv_b, lam_pack, g_subln, pool_base, lam_init):
    db, n_pages = page_table.shape
    page = ck_a.shape[2]
    past = n_pages * page

    per_b = lambda b, j, pt: (b, 0, 0)
    paged = lambda b, j, pt: (pool_base + pt[b, n_pages - jnp.maximum(j, 1)], 0, 0)
    cache_spec = pl.BlockSpec((None, W_GROUP, page), paged)
    new_spec = pl.BlockSpec((None, W_GROUP, page), per_b)
    q_spec = pl.BlockSpec((None, ROWS_S, W_GROUP), per_b)
    out_spec = pl.BlockSpec((None, Q_PAD, W_GROUP), per_b)
    out = jax.ShapeDtypeStruct((db, Q_PAD, W_GROUP), F32)
    return pl.pallas_call(
        functools.partial(_sample_kernel, past=past, page=page, lam_init=lam_init),
        grid_spec=pltpu.PrefetchScalarGridSpec(
            num_scalar_prefetch=1,
            grid=(db, n_pages + 1),
            in_specs=[q_spec, q_spec, new_spec, new_spec, new_spec, new_spec,
                      cache_spec, cache_spec, cache_spec, cache_spec,
                      pl.BlockSpec(lam_pack.shape, lambda b, j, pt: (0, 0, 0)),
                      pl.BlockSpec(g_subln.shape, lambda b, j, pt: (0, 0))],
            out_specs=[out_spec, out_spec],
            scratch_shapes=[pltpu.VMEM((ROWS_S, 1), F32), pltpu.VMEM((ROWS_S, 1), F32),
                            pltpu.VMEM((ROWS_S, DV_A), F32), pltpu.VMEM((ROWS_S, 1), F32),
                            pltpu.VMEM((ROWS_S, W_GROUP), F32)]),
        out_shape=[out, out],
        compiler_params=pltpu.CompilerParams(dimension_semantics=("parallel", "arbitrary"),
                                             vmem_limit_bytes=VMEM_LIMIT),
        name="sample_attention",
    )(page_table, qbd_a, qbd_b, kn_a, vn_a, kn_b, vn_b, ck_a, cv_a, ck_b, cv_b, lam_pack, g_subln)


def _sigmoid(x):
    return 1.0 / (1.0 + jnp.exp(-x))


def _merge_kernel(x_ref, oa_ref, ob_ref, gate_ref, p_ref, wo_ref, gp_ref, wg_ref, wp_ref, gf_ref,
                  y_ref, *, final):
    gate = gate_ref[...]
    sg = gate * _sigmoid(gate)
    ua = (oa_ref[...] * sg[:, :W_GROUP]).astype(BF16)
    ub = (ob_ref[...] * sg[:, W_GROUP:]).astype(BF16)
    x1 = (x_ref[...] + jnp.dot(ua, wo_ref[:W_GROUP, :], preferred_element_type=F32)
          + jnp.dot(ub, wo_ref[W_GROUP:, :], preferred_element_type=F32))
    hn = _rms(x1, gp_ref[...]).astype(BF16)
    g = _sigmoid(jnp.dot(hn, wg_ref[...], preferred_element_type=F32))
    pe = jnp.dot(p_ref[...].astype(BF16), wp_ref[...], preferred_element_type=F32)
    x2 = x1 + g * pe
    y_ref[...] = _rms(x2, gf_ref[...]) if final else x2


def _merge(x, o_a, o_b, gate, p, w_out, g_ple, w_gate, w_proj, g_final, tm, final):
    n, d = x.shape
    assert n % tm == 0
    row = lambda i: (i, 0)
    fixed = lambda i: (0, 0)
    full = lambda a: pl.BlockSpec(a.shape, fixed)
    g_ple = g_ple.reshape(1, d)
    g_final = g_final.reshape(1, d)
    return pl.pallas_call(
        functools.partial(_merge_kernel, final=final),
        grid=(n // tm,),
        in_specs=[pl.BlockSpec((tm, d), row), pl.BlockSpec((tm, W_GROUP), row),
                  pl.BlockSpec((tm, W_GROUP), row), pl.BlockSpec((tm, MIX), row),
                  pl.BlockSpec((tm, p.shape[1]), row), full(w_out), full(g_ple), full(w_gate),
                  full(w_proj), full(g_final)],
        out_specs=pl.BlockSpec((tm, d), row),
        out_shape=jax.ShapeDtypeStruct((n, d), F32),
        compiler_params=pltpu.CompilerParams(dimension_semantics=("parallel",),
                                             vmem_limit_bytes=VMEM_LIMIT),
        name="merge",
    )(x, o_a, o_b, gate, p, w_out, g_ple, w_gate, w_proj, g_final)


def _block_diag_queries(q, group_width):
    n_groups = W_GROUP // group_width
    lane_group = jnp.arange(W_GROUP, dtype=jnp.int32) // group_width
    mask = (lane_group[None, :] == jnp.arange(n_groups, dtype=jnp.int32)[:, None]).astype(q.dtype)
    out = q[:, None, :, :] * mask[None, :, None, :]
    return out.reshape(q.shape[0], n_groups * Q_PAD, W_GROUP)


def _features_major(a, db, ds, page):
    a = a.reshape(db, ds, W_GROUP).transpose(0, 2, 1)
    return jnp.pad(a, ((0, 0), (0, 0), (0, page - ds)))


def kernel(x_prompt, x_sample, cache_k_diff, cache_v_diff, cache_k_sb, cache_v_sb, page_table,
           p_prompt, p_sample, g_norm, w_in, lambda_q1, lambda_k1, lambda_q2, lambda_k2,
           g_subln, w_out, g_ple, w_ple_gate, w_ple_proj, g_final):
    b, t, d = x_prompt.shape
    db, ds, _ = x_sample.shape
    depth, n_pool, page = cache_k_diff.shape[:3]
    assert ds <= Q_PAD

    xp = x_prompt.reshape(b * t, d)
    xs = x_sample.reshape(db * ds, d)
    n_all = depth * n_pool
    pool_k_a = cache_k_diff.transpose(0, 1, 3, 4, 5, 2).reshape(n_all, W_GROUP, page)
    pool_v_a = cache_v_diff.reshape(n_all, page * H_A, DV_A)
    pool_k_b = cache_k_sb.transpose(0, 1, 3, 4, 2).reshape(n_all, W_GROUP, page)
    pool_v_b = cache_v_sb.transpose(0, 1, 3, 4, 2).reshape(n_all, W_GROUP, page)
    assert page * H_A == W_GROUP and DV_A == page

    new_p = [[] for _ in range(4)]
    new_s = [[] for _ in range(4)]
    for l in range(depth):
        lam_init = 0.8 - 0.6 * math.exp(-0.3 * l)
        final = l == depth - 1
        w_in_l = w_in[l].astype(BF16)
        w_out_l = w_out[l].astype(BF16)
        w_gate_l = w_ple_gate[l].astype(BF16)
        w_proj_l = w_ple_proj[l].astype(BF16)
        lam_pack = jnp.stack([lambda_q1[l], lambda_k1[l], lambda_q2[l], lambda_k2[l]])[:, None, :]
        g_sub = g_subln[l].reshape(1, DV_A)

        zb, ka, va, kb, vb, gate = _proj(xp, g_norm[l], w_in_l, tm=512)
        zb3 = zb.reshape(b, t, N_GROUPS_QKV * W_GROUP)
        o_a = _prompt_attention(functools.partial(_diff_prompt_kernel, lam_init=lam_init),
                                zb3, [lam_pack, g_sub], 0, 1, 2, "diff_prompt")
        o_b = _prompt_attention(_sb_prompt_kernel, zb3, [], 3, 4, 5, "sb_prompt")
        xp = _merge(xp, o_a.reshape(b * t, W_GROUP), o_b.reshape(b * t, W_GROUP), gate,
                    p_prompt[l].reshape(b * t, -1), w_out_l, g_ple[l], w_gate_l, w_proj_l,
                    g_final, 512, final)
        for dst, a in zip(new_p, (ka, va, kb, vb)):
            dst.append(a)

        zb, ka, va, kb, vb, gate = _proj(xs, g_norm[l], w_in_l, tm=db * ds)
        zs = zb.reshape(db, ds, N_GROUPS_QKV, W_GROUP)
        q_pad = lambda i: jnp.pad(zs[:, :, i], ((0, 0), (0, Q_PAD - ds), (0, 0)))
        vn_a = jnp.pad(va.reshape(db, ds, W_GROUP), ((0, 0), (0, page - ds), (0, 0)))
        so_a, so_b = _sample_attention(
            page_table, _block_diag_queries(q_pad(0), DH_A), _block_diag_queries(q_pad(3), DH_B),
            _features_major(ka, db, ds, page), vn_a.reshape(db, page * H_A, DV_A),
            _features_major(kb, db, ds, page), _features_major(vb, db, ds, page),
            pool_k_a, pool_v_a, pool_k_b, pool_v_b, lam_pack, g_sub, l * n_pool, lam_init)
        xs = _merge(xs, so_a[:, :ds].reshape(db * ds, W_GROUP), so_b[:, :ds].reshape(db * ds, W_GROUP),
                    gate, p_sample[l].reshape(db * ds, -1), w_out_l, g_ple[l], w_gate_l, w_proj_l,
                    g_final, db * ds, final)
        for dst, a in zip(new_s, (ka, va, kb, vb)):
            dst.append(a)

    def stacked(parts, lead, tail):
        arr = jnp.stack(parts) if depth > 1 else parts[0]
        return arr.reshape(depth, *lead, *tail)

    kd, vd = (H_A, 2, DH_A), (H_A, DV_A)
    ksb = (H_B, DH_B)
    return (xp.reshape(b, t, d), xs.reshape(db, ds, d),
            stacked(new_p[0], (b, t), kd), stacked(new_p[1], (b, t), vd),
            stacked(new_p[2], (b, t), ksb), stacked(new_p[3], (b, t), ksb),
            stacked(new_s[0], (db, ds), kd), stacked(new_s[1], (db, ds), vd),
            stacked(new_s[2], (db, ds), ksb), stacked(new_s[3], (db, ds), ksb))
```

```python
import functools
import math

import jax
import jax.numpy as jnp
from jax import lax
from jax.experimental import pallas as pl
from jax.experimental.pallas import tpu as pltpu

F32 = jnp.float32
BF16 = jnp.bfloat16

H_A = 4
DH_A = 64
DV_A = 2 * DH_A
H_B = 8
DH_B = 64
EPS = 1e-6
NEG_INF = -1e30
W_GROUP = H_A * DV_A
N_GROUPS_QKV = 6
TOKEN_MINOR_GROUPS = (1, 4, 5)
MIX = 2 * W_GROUP

LANES = 128
SUBLANES = 8
HALF = LANES // 2
VMEM_LIMIT = 56 * 1024 * 1024

ALIBI_LOG2_STEP = 8 // H_A
assert 8 % H_A == 0

TQ = 512
TK_DIFF = 512
TK_SB = 256
ROW_CHUNK = 128
STREAMS = 2
Q_PAD = SUBLANES
ROWS_S = 2 * H_A * Q_PAD
assert ROWS_S == H_B * Q_PAD
PAGES_PER_STEP = 8


def _pow2_neg(e):
    return lax.bitcast_convert_type((127 - e) << 23, F32)


def _rms(x, g):
    return x * lax.rsqrt(jnp.mean(x * x, axis=-1, keepdims=True) + EPS) * g


def _nt_dot(a, b):
    return lax.dot_general(a, b, (((1,), (1,)), ((), ())), preferred_element_type=F32)


def _dot(a, b):
    return jnp.dot(a, b, preferred_element_type=F32)


def _lambda(lam_ref, lam_init):
    t1 = jnp.sum(lam_ref[0] * lam_ref[1], axis=-1, keepdims=True)
    t2 = jnp.sum(lam_ref[2] * lam_ref[3], axis=-1, keepdims=True)
    return jnp.exp(t1) - jnp.exp(t2) + lam_init


def _softplus(z):
    return jnp.maximum(z, 0.0) + jnp.log(1.0 + jnp.exp(-jnp.abs(z)))


def _split_bf16(x):
    hi = x.astype(BF16)
    lo = (x - hi.astype(F32)).astype(BF16)
    return hi, lo


def _strict_lower_ones(n):
    r = lax.broadcasted_iota(jnp.int32, (n, n), 0)
    c = lax.broadcasted_iota(jnp.int32, (n, n), 1)
    return jnp.where(r > c, 1.0, 0.0).astype(BF16)


def _proj_kernel(x_ref, g_ref, w_ref, wt_ref, zb_ref, kta_ref, va_ref, ktb_ref, vtb_ref, gate_ref):
    h = _rms(x_ref[...], g_ref[...]).astype(BF16)
    tm = h.shape[0]

    def cols(i):
        return _dot(h, w_ref[:, i * W_GROUP:(i + 1) * W_GROUP])

    for i in range(N_GROUPS_QKV):
        z = cols(i)
        if i == 2:
            for hh in range(H_A):
                va_ref[pl.ds(hh, tm, stride=H_A), :] = z[:, hh * DV_A:(hh + 1) * DV_A]
        if i in (0, 3):
            z = z * (DH_A ** -0.5)
        zb_ref[:, i * W_GROUP:(i + 1) * W_GROUP] = z.astype(BF16)
    for slot, ref in enumerate((kta_ref, ktb_ref, vtb_ref)):
        ref[...] = _nt_dot(wt_ref[slot * W_GROUP:(slot + 1) * W_GROUP, :], h)
    for i in range(2):
        gate_ref[:, i * W_GROUP:(i + 1) * W_GROUP] = cols(N_GROUPS_QKV + i)


def _proj(x, g, w_bf16, wt_bf16, tm, rows_per_batch):
    n, d = x.shape
    d_in = w_bf16.shape[1]
    assert n % rows_per_batch == 0 and rows_per_batch % tm == 0
    assert d_in == (N_GROUPS_QKV + 2) * W_GROUP
    tiles_per_batch = rows_per_batch // tm
    row = lambda i: (i, 0)
    fixed = lambda i: (0, 0)
    tok_minor = lambda i: (i // tiles_per_batch, 0, i % tiles_per_batch)
    kt = jax.ShapeDtypeStruct((n // rows_per_batch, W_GROUP, rows_per_batch), F32)
    kt_spec = pl.BlockSpec((None, W_GROUP, tm), tok_minor)
    return pl.pallas_call(
        _proj_kernel,
        grid=(n // tm,),
        in_specs=[pl.BlockSpec((tm, d), row), pl.BlockSpec((1, d), fixed),
                  pl.BlockSpec((d, d_in), fixed), pl.BlockSpec(wt_bf16.shape, fixed)],
        out_specs=[pl.BlockSpec((tm, N_GROUPS_QKV * W_GROUP), row), kt_spec,
                   pl.BlockSpec((tm * H_A, DV_A), row), kt_spec, kt_spec,
                   pl.BlockSpec((tm, MIX), row)],
        out_shape=[jax.ShapeDtypeStruct((n, N_GROUPS_QKV * W_GROUP), BF16), kt,
                   jax.ShapeDtypeStruct((n * H_A, DV_A), F32), kt, kt,
                   jax.ShapeDtypeStruct((n, MIX), F32)],
        compiler_params=pltpu.CompilerParams(dimension_semantics=("parallel",),
                                             vmem_limit_bytes=VMEM_LIMIT),
        name="proj",
    )(x, g.reshape(1, d), w_bf16, wt_bf16)


def _stack_halves(x):
    lane = lax.broadcasted_iota(jnp.int32, x.shape, 1)
    zero = jnp.zeros_like(x)
    return jnp.concatenate([jnp.where(lane < HALF, x, zero), jnp.where(lane >= HALF, x, zero)], axis=0)


def _row_chunks():
    return [(r0, r0 + ROW_CHUNK) for r0 in range(0, 2 * TQ, ROW_CHUNK)]


def _causal_iotas(r0, tk):
    r = (lax.broadcasted_iota(jnp.int32, (ROW_CHUNK, tk), 0) + r0) & (TQ - 1)
    c = lax.broadcasted_iota(jnp.int32, (ROW_CHUNK, tk), 1)
    return r, c


def _lane_replicated(x):
    return jnp.broadcast_to(x, (x.shape[0], LANES))


def _lane_tiled(x, width):
    return jnp.concatenate([x] * (width // LANES), axis=1)


def _stream_lanes():
    return [slice(s * LANES, (s + 1) * LANES) for s in range(STREAMS)]


def _diff_prompt_kernel(q_ref, k_ref, v_ref, lam_ref, g_ref, o_ref, *, lam_init):
    tk = TK_DIFF
    n_sub = TQ // tk
    qi = pl.program_id(2)
    lanes = _stream_lanes()
    chunks = _row_chunks()
    qs = [_stack_halves(q_ref[:, ln]) for ln in lanes]
    heads = [pl.program_id(1) * STREAMS + s for s in range(STREAMS)]
    slope = [_pow2_neg(jnp.full((1, tk), ALIBI_LOG2_STEP, jnp.int32) * (h + 1)) for h in heads]
    col = lax.broadcasted_iota(jnp.int32, (1, tk), 1).astype(F32)
    ones = jnp.ones((tk, LANES), BF16)

    def block(kj, carry, key_offset, diag_sub=None):
        start = pl.multiple_of(kj * tk, tk)
        s_all = [_nt_dot(qs[s], k_ref[pl.ds(start, tk), ln]) for s, ln in enumerate(lanes)]
        vs = [jnp.concatenate([v_ref[pl.ds(start, tk), ln], ones], axis=1) for ln in lanes]
        bias = [sl * (col + key_offset) for sl in slope]

        def scores(s, r0, r1):
            x = s_all[s][r0:r1] + bias[s]
            if diag_sub is not None:
                r, c = _causal_iotas(r0, tk)
                x = jnp.where(c + diag_sub * tk <= r, x, NEG_INF)
            return x

        m_new = [jnp.concatenate(
            [jnp.maximum(carry[s][0][r0:r1],
                         _lane_replicated(jnp.max(scores(s, r0, r1), axis=-1, keepdims=True)))
             for r0, r1 in chunks], axis=0) for s in range(STREAMS)]
        p = [jnp.concatenate(
            [jnp.exp(scores(s, r0, r1) - _lane_tiled(m_new[s][r0:r1], tk)).astype(BF16)
             for r0, r1 in chunks], axis=0) for s in range(STREAMS)]
        out = []
        for s in range(STREAMS):
            m, acc = carry[s]
            alpha = jnp.exp(m - m_new[s])
            out.append((m_new[s], _lane_tiled(alpha, 2 * LANES) * acc + _dot(p[s], vs[s])))
        return tuple(out)

    carry = tuple((jnp.full((2 * TQ, LANES), NEG_INF, F32), jnp.zeros((2 * TQ, 2 * LANES), F32))
                  for _ in range(STREAMS))
    carry = lax.fori_loop(
        0, qi * n_sub, lambda kj, c: block(kj, c, (kj * tk - qi * TQ).astype(F32)), carry)
    for sub in range(n_sub):
        carry = block(qi * n_sub + sub, carry, float(sub * tk), diag_sub=sub)
    lam = _lambda(lam_ref, lam_init)
    for ln, (_, acc) in zip(lanes, carry):
        o = acc[:, :LANES] / acc[:, LANES:]
        d = o[:TQ] - lam * o[TQ:]
        o_ref[:, ln] = _rms(d, g_ref[...]) * (1.0 - lam_init)


def _sb_prompt_kernel(q_ref, k_ref, v_ref, o_ref):
    tk = TK_SB
    n_sub = TQ // tk
    qi = pl.program_id(2)
    lanes = _stream_lanes()
    chunks = _row_chunks()
    half = len(chunks) // 2
    qs = [_stack_halves(q_ref[:, ln]) for ln in lanes]
    tri = _strict_lower_ones(tk)

    def block(kj, carry, diag_sub=None):
        start = pl.multiple_of(kj * tk, tk)
        z_all = [_nt_dot(qs[s], k_ref[pl.ds(start, tk), ln]) for s, ln in enumerate(lanes)]
        vs = [_stack_halves(v_ref[pl.ds(start, tk), ln]) for ln in lanes]
        masks = [None] * len(chunks)
        if diag_sub is not None:
            for i, (r0, _) in enumerate(chunks):
                r, c = _causal_iotas(r0, tk)
                masks[i] = c + diag_sub * tk < r
        log_beta, hi, lo, sums = [], [], [], []
        for s in range(STREAMS):
            lb_s, hi_s, lo_s, sum_s = [], [], [], []
            for (r0, r1), mask in zip(chunks, masks):
                z = z_all[s][r0:r1]
                sp = _softplus(z)
                log_rest = -sp
                if mask is not None:
                    log_rest = jnp.where(mask, log_rest, 0.0)
                h_c, l_c = _split_bf16(log_rest)
                lb_s.append(z - sp)
                hi_s.append(h_c)
                lo_s.append(l_c)
                sum_s.append(_lane_replicated(jnp.sum(log_rest, axis=-1, keepdims=True)))
            log_beta.append(lb_s)
            hi.append(jnp.concatenate(hi_s, axis=0))
            lo.append(jnp.concatenate(lo_s, axis=0))
            sums.append(jnp.concatenate(sum_s, axis=0))
        after = [_dot(hi[s], tri) + _dot(lo[s], tri) for s in range(STREAMS)]
        out = []
        for s in range(STREAMS):
            run, acc = carry[s]
            w_out = []
            for (r0, r1), mask, lb in zip(chunks, masks, log_beta[s]):
                w = jnp.exp(lb + after[s][r0:r1] + _lane_tiled(run[r0:r1], tk))
                if mask is not None:
                    w = jnp.where(mask, w, 0.0)
                w_out.append(w.astype(BF16))
            wb = jnp.concatenate([jnp.concatenate(w_out[:half], axis=0),
                                  jnp.concatenate(w_out[half:], axis=0)], axis=1)
            out.append((run + sums[s], acc + _dot(wb, vs[s])))
        return tuple(out)

    carry = tuple((jnp.zeros((2 * TQ, LANES), F32), jnp.zeros((TQ, LANES), F32))
                  for _ in range(STREAMS))
    for sub in reversed(range(n_sub)):
        carry = block(qi * n_sub + sub, carry, diag_sub=sub)
    carry = lax.fori_loop(0, qi * n_sub, lambda i, c: block(qi * n_sub - 1 - i, c), carry)
    for ln, (_, acc) in zip(lanes, carry):
        o_ref[:, ln] = acc


def _prompt_attention(body, zb, extra, q_blk, k_blk, v_blk, name):
    b, t, _ = zb.shape
    assert t % TQ == 0 and TQ % TK_DIFF == 0 and TQ % TK_SB == 0
    width = STREAMS * LANES
    assert W_GROUP % width == 0
    n_grp = W_GROUP // width
    extra_specs = [pl.BlockSpec(e.shape, lambda bi, g, qi, nd=e.ndim: (0,) * nd) for e in extra]
    return pl.pallas_call(
        body,
        grid=(b, n_grp, t // TQ),
        in_specs=[pl.BlockSpec((None, TQ, width), lambda bi, g, qi: (bi, qi, q_blk * n_grp + g)),
                  pl.BlockSpec((None, t, width), lambda bi, g, qi: (bi, 0, k_blk * n_grp + g)),
                  pl.BlockSpec((None, t, width), lambda bi, g, qi: (bi, 0, v_blk * n_grp + g))]
        + extra_specs,
        out_specs=pl.BlockSpec((None, TQ, width), lambda bi, g, qi: (bi, qi, g)),
        out_shape=jax.ShapeDtypeStruct((b, t, W_GROUP), F32),
        compiler_params=pltpu.CompilerParams(
            dimension_semantics=("parallel", "parallel", "parallel"),
            vmem_limit_bytes=VMEM_LIMIT),
        name=name,
    )(zb, zb, zb, *extra)


def _sample_kernel(pt_ref, qa_ref, qb_ref, kna_ref, vna_ref, knb_ref, vnb_ref, *rest,
                   past, page, lam_init):
    del pt_ref
    n_cached = 4 * PAGES_PER_STEP
    pools = [rest[i * PAGES_PER_STEP:(i + 1) * PAGES_PER_STEP] for i in range(4)]
    lam_ref, g_ref, oa_ref, ob_ref, m_sc, l_sc, acca_sc, run_sc, accb_sc = rest[n_cached:]
    j = pl.program_id(1)
    n_pages = past // page
    rows_h = 2 * Q_PAD
    r = lax.broadcasted_iota(jnp.int32, (ROWS_S, page), 0)
    c = lax.broadcasted_iota(jnp.int32, (ROWS_S, page), 1)
    col = c.astype(F32)
    q_of_row = r & (Q_PAD - 1)
    slope = _pow2_neg(ALIBI_LOG2_STEP * ((r // rows_h) + 1))
    tri = _strict_lower_ones(page)

    def attend(blocks, masked):
        qa = qa_ref[...]
        qb = qb_ref[...]
        scores = [_dot(qa, blk[0][...].astype(BF16)) for blk in blocks]
        logits = [_dot(qb, blk[2][...].astype(BF16)) for blk in blocks]

        for i, blk in enumerate(blocks):
            scores[i] = scores[i] + slope * (col + blk[4])
            if masked:
                scores[i] = jnp.where(c <= q_of_row, scores[i], NEG_INF)
        m = m_sc[...]
        m_new = jnp.maximum(m, jnp.max(functools.reduce(jnp.maximum, scores), axis=-1, keepdims=True))
        alpha = jnp.exp(m - m_new)
        probs = [jnp.exp(s - m_new) for s in scores]
        m_sc[...] = m_new
        l_sc[...] = alpha * l_sc[...] + jnp.sum(functools.reduce(jnp.add, probs), axis=-1,
                                                keepdims=True)
        probs = [p.astype(BF16) for p in probs]

        mask = (c < q_of_row) if masked else None
        log_beta, splits, sums = [], [], []
        for z in logits:
            sp = _softplus(z)
            log_rest = -sp
            if masked:
                log_rest = jnp.where(mask, log_rest, 0.0)
            log_beta.append(z - sp)
            splits.append(_split_bf16(log_rest))
            sums.append(jnp.sum(log_rest, axis=-1, keepdims=True))
        after = [_dot(hi, tri) + _dot(lo, tri) for hi, lo in splits]

        for h in range(H_A):
            rows = slice(h * rows_h, (h + 1) * rows_h)
            upd = None
            for blk, p in zip(blocks, probs):
                v_h = blk[1][pl.ds(h, page, stride=H_A), :].astype(BF16)
                d = _dot(p[rows], v_h)
                upd = d if upd is None else upd + d
            acca_sc[rows, :] = alpha[rows] * acca_sc[rows, :] + upd

        run = run_sc[...]
        weights = []
        for lb, aft, rs in zip(log_beta, after, sums):
            w = jnp.exp(lb + aft + run)
            if masked:
                w = jnp.where(mask, w, 0.0)
            weights.append(w.astype(BF16))
            run = run + rs
        run_sc[...] = run
        upd = None
        for blk, w in zip(blocks, weights):
            d = _nt_dot(w, blk[3][...].astype(BF16))
            upd = d if upd is None else upd + d
        accb_sc[...] = accb_sc[...] + upd

    @pl.when(j == 0)
    def _():
        m_sc[...] = jnp.full(m_sc.shape, NEG_INF, F32)
        l_sc[...] = jnp.zeros(l_sc.shape, F32)
        acca_sc[...] = jnp.zeros(acca_sc.shape, F32)
        run_sc[...] = jnp.zeros(run_sc.shape, F32)
        accb_sc[...] = jnp.zeros(accb_sc.shape, F32)
        attend([(kna_ref, vna_ref, knb_ref, vnb_ref, 0.0)], True)

    @pl.when(j > 0)
    def _():
        newest = n_pages - 1 - (j - 1) * PAGES_PER_STEP
        attend([(pools[0][i], pools[1][i], pools[2][i], pools[3][i],
                 ((newest - i) * page - past).astype(F32)) for i in range(PAGES_PER_STEP)], False)

    @pl.when(j == n_pages // PAGES_PER_STEP)
    def _():
        o = acca_sc[...] / l_sc[...]
        lam = _lambda(lam_ref, lam_init)
        for h in range(H_A):
            d = (o[h * rows_h:h * rows_h + Q_PAD]
                 - lam * o[h * rows_h + Q_PAD:(h + 1) * rows_h])
            oa_ref[:, h * DV_A:(h + 1) * DV_A] = _rms(d, g_ref[...]) * (1.0 - lam_init)
        acc = accb_sc[...]
        lane_head = lax.broadcasted_iota(jnp.int32, (Q_PAD, W_GROUP), 1) // DH_B
        out = jnp.zeros((Q_PAD, W_GROUP), F32)
        for h in range(H_B):
            out = out + jnp.where(lane_head == h, acc[h * Q_PAD:(h + 1) * Q_PAD], 0.0)
        ob_ref[...] = out


def _sample_attention(page_table, qbd_a, qbd_b, kn_a, vn_a, kn_b, vn_b,
                      pool_arrays, lam_pack, g_subln, pool_base, lam_init):
    db, n_pages = page_table.shape
    page = pool_arrays[0].shape[2]
    past = n_pages * page
    assert n_pages % PAGES_PER_STEP == 0

    per_b = lambda b, j, pt: (b, 0, 0)

    def paged(i):
        def index(b, j, pt):
            newest = n_pages - 1 - (jnp.maximum(j, 1) - 1) * PAGES_PER_STEP
            return (pool_base + pt[b, newest - i], 0, 0)
        return pl.BlockSpec((None, W_GROUP, page), index)

    new_spec = pl.BlockSpec((None, W_GROUP, page), per_b)
    q_spec = pl.BlockSpec((None, ROWS_S, W_GROUP), per_b)
    out_spec = pl.BlockSpec((None, Q_PAD, W_GROUP), per_b)
    out = jax.ShapeDtypeStruct((db, Q_PAD, W_GROUP), F32)
    cache_specs = [paged(i) for _ in pool_arrays for i in range(PAGES_PER_STEP)]
    cache_args = [a for a in pool_arrays for _ in range(PAGES_PER_STEP)]
    return pl.pallas_call(
        functools.partial(_sample_kernel, past=past, page=page, lam_init=lam_init),
        grid_spec=pltpu.PrefetchScalarGridSpec(
            num_scalar_prefetch=1,
            grid=(db, n_pages // PAGES_PER_STEP + 1),
            in_specs=[q_spec, q_spec, new_spec, new_spec, new_spec, new_spec] + cache_specs
            + [pl.BlockSpec(lam_pack.shape, lambda b, j, pt: (0, 0, 0)),
               pl.BlockSpec(g_subln.shape, lambda b, j, pt: (0, 0))],
            out_specs=[out_spec, out_spec],
            scratch_shapes=[pltpu.VMEM((ROWS_S, 1), F32), pltpu.VMEM((ROWS_S, 1), F32),
                            pltpu.VMEM((ROWS_S, DV_A), F32), pltpu.VMEM((ROWS_S, 1), F32),
                            pltpu.VMEM((ROWS_S, W_GROUP), F32)]),
        out_shape=[out, out],
        compiler_params=pltpu.CompilerParams(dimension_semantics=("parallel", "arbitrary"),
                                             vmem_limit_bytes=VMEM_LIMIT),
        name="sample_attention",
    )(page_table, qbd_a, qbd_b, kn_a, vn_a, kn_b, vn_b, *cache_args, lam_pack, g_subln)


def _sigmoid(x):
    return 1.0 / (1.0 + jnp.exp(-x))


def _merge_kernel(x_ref, oa_ref, ob_ref, gate_ref, p_ref, wo_ref, gp_ref, wg_ref, wp_ref, gf_ref,
                  y_ref, *, final):
    gate = gate_ref[...]
    sg = gate * _sigmoid(gate)
    ua = (oa_ref[...] * sg[:, :W_GROUP]).astype(BF16)
    ub = (ob_ref[...] * sg[:, W_GROUP:]).astype(BF16)
    x1 = x_ref[...] + _dot(ua, wo_ref[:W_GROUP, :]) + _dot(ub, wo_ref[W_GROUP:, :])
    hn = _rms(x1, gp_ref[...]).astype(BF16)
    g = _sigmoid(_dot(hn, wg_ref[...]))
    pe = _dot(p_ref[...].astype(BF16), wp_ref[...])
    x2 = x1 + g * pe
    y_ref[...] = _rms(x2, gf_ref[...]) if final else x2


def _merge(x, o_a, o_b, gate, p, w_out, g_ple, w_gate, w_proj, g_final, tm, final):
    n, d = x.shape
    assert n % tm == 0
    row = lambda i: (i, 0)
    fixed = lambda i: (0, 0)
    full = lambda a: pl.BlockSpec(a.shape, fixed)
    g_ple = g_ple.reshape(1, d)
    g_final = g_final.reshape(1, d)
    return pl.pallas_call(
        functools.partial(_merge_kernel, final=final),
        grid=(n // tm,),
        in_specs=[pl.BlockSpec((tm, d), row), pl.BlockSpec((tm, W_GROUP), row),
                  pl.BlockSpec((tm, W_GROUP), row), pl.BlockSpec((tm, MIX), row),
                  pl.BlockSpec((tm, p.shape[1]), row), full(w_out), full(g_ple), full(w_gate),
                  full(w_proj), full(g_final)],
        out_specs=pl.BlockSpec((tm, d), row),
        out_shape=jax.ShapeDtypeStruct((n, d), F32),
        compiler_params=pltpu.CompilerParams(dimension_semantics=("parallel",),
                                             vmem_limit_bytes=VMEM_LIMIT),
        name="merge",
    )(x, o_a, o_b, gate, p, w_out, g_ple, w_gate, w_proj, g_final)


def _block_diag_queries(q, group_width):
    n_groups = W_GROUP // group_width
    lane_group = jnp.arange(W_GROUP, dtype=jnp.int32) // group_width
    mask = (lane_group[None, :] == jnp.arange(n_groups, dtype=jnp.int32)[:, None]).astype(q.dtype)
    out = q[:, None, :, :] * mask[None, :, None, :]
    return out.reshape(q.shape[0], n_groups * Q_PAD, W_GROUP)


def _tokens_major(kt, lead, tail):
    nb, _, rows = kt.shape
    a = jnp.moveaxis(kt.reshape(nb, *tail, rows), -1, 1)
    return a.reshape(*lead, *tail)


def kernel(x_prompt, x_sample, cache_k_diff, cache_v_diff, cache_k_sb, cache_v_sb, page_table,
           p_prompt, p_sample, g_norm, w_in, lambda_q1, lambda_k1, lambda_q2, lambda_k2,
           g_subln, w_out, g_ple, w_ple_gate, w_ple_proj, g_final):
    b, t, d = x_prompt.shape
    db, ds, _ = x_sample.shape
    depth, n_pool, page = cache_k_diff.shape[:3]
    assert ds <= Q_PAD and page * H_A == W_GROUP and DV_A == page

    xp = x_prompt.reshape(b * t, d)
    xs = x_sample.reshape(db * ds, d)
    n_all = depth * n_pool
    pools = [cache_k_diff.transpose(0, 1, 3, 4, 5, 2).reshape(n_all, W_GROUP, page),
             cache_v_diff.reshape(n_all, page * H_A, DV_A),
             cache_k_sb.transpose(0, 1, 3, 4, 2).reshape(n_all, W_GROUP, page),
             cache_v_sb.transpose(0, 1, 3, 4, 2).reshape(n_all, W_GROUP, page)]

    kd, vd, ksb = (H_A, 2, DH_A), (H_A, DV_A), (H_B, DH_B)
    new_p = [[] for _ in range(4)]
    new_s = [[] for _ in range(4)]
    for l in range(depth):
        lam_init = 0.8 - 0.6 * math.exp(-0.3 * l)
        final = l == depth - 1
        w_in_l = w_in[l].astype(BF16)
        wt_in_l = jnp.concatenate(
            [w_in_l[:, i * W_GROUP:(i + 1) * W_GROUP].T for i in TOKEN_MINOR_GROUPS], axis=0)
        w_out_l = w_out[l].astype(BF16)
        w_gate_l = w_ple_gate[l].astype(BF16)
        w_proj_l = w_ple_proj[l].astype(BF16)
        lam_pack = jnp.stack([lambda_q1[l], lambda_k1[l], lambda_q2[l], lambda_k2[l]])[:, None, :]
        g_sub = g_subln[l].reshape(1, DV_A)

        zb, kta, va, ktb, vtb, gate = _proj(xp, g_norm[l], w_in_l, wt_in_l, 512, t)
        zb3 = zb.reshape(b, t, N_GROUPS_QKV * W_GROUP)
        o_a = _prompt_attention(functools.partial(_diff_prompt_kernel, lam_init=lam_init),
                                zb3, [lam_pack, g_sub], 0, 1, 2, "diff_prompt")
        o_b = _prompt_attention(_sb_prompt_kernel, zb3, [], 3, 4, 5, "sb_prompt")
        xp = _merge(xp, o_a.reshape(b * t, W_GROUP), o_b.reshape(b * t, W_GROUP), gate,
                    p_prompt[l].reshape(b * t, -1), w_out_l, g_ple[l], w_gate_l, w_proj_l,
                    g_final, 512, final)
        new_p[0].append(_tokens_major(kta, (b, t), kd))
        new_p[1].append(va.reshape(b, t, *vd))
        new_p[2].append(_tokens_major(ktb, (b, t), ksb))
        new_p[3].append(_tokens_major(vtb, (b, t), ksb))

        zb, kta, va, ktb, vtb, gate = _proj(xs, g_norm[l], w_in_l, wt_in_l, db * ds, db * ds)
        zs = zb.reshape(db, ds, N_GROUPS_QKV, W_GROUP)
        q_pad = lambda i: jnp.pad(zs[:, :, i], ((0, 0), (0, Q_PAD - ds), (0, 0)))
        per_b = lambda kt: jnp.pad(kt.reshape(W_GROUP, db, ds).transpose(1, 0, 2),
                                   ((0, 0), (0, 0), (0, page - ds)))
        vn_a = jnp.pad(va.reshape(db, ds, H_A, DV_A), ((0, 0), (0, page - ds), (0, 0), (0, 0)))
        so_a, so_b = _sample_attention(
            page_table, _block_diag_queries(q_pad(0), DH_A), _block_diag_queries(q_pad(3), DH_B),
            per_b(kta), vn_a.reshape(db, page * H_A, DV_A), per_b(ktb), per_b(vtb),
            pools, lam_pack, g_sub, l * n_pool, lam_init)
        xs = _merge(xs, so_a[:, :ds].reshape(db * ds, W_GROUP), so_b[:, :ds].reshape(db * ds, W_GROUP),
                    gate, p_sample[l].reshape(db * ds, -1), w_out_l, g_ple[l], w_gate_l, w_proj_l,
                    g_final, db * ds, final)
        new_s[0].append(_tokens_major(kta, (db, ds), kd))
        new_s[1].append(va.reshape(db, ds, *vd))
        new_s[2].append(_tokens_major(ktb, (db, ds), ksb))
        new_s[3].append(_tokens_major(vtb, (db, ds), ksb))

    stacked = lambda parts: jnp.stack(parts) if depth > 1 else parts[0][None]
    return (xp.reshape(b, t, d), xs.reshape(db, ds, d),
            *[stacked(parts) for parts in new_p], *[stacked(parts) for parts in new_s])
```

```python
import functools
import math

import jax
import jax.numpy as jnp
from jax import lax
from jax.experimental import pallas as pl
from jax.experimental.pallas import tpu as pltpu

F32 = jnp.float32
BF16 = jnp.bfloat16

H_A = 4
DH_A = 64
DV_A = 2 * DH_A
H_B = 8
DH_B = 64
EPS = 1e-6
NEG_INF = -1e30
LOG2_E = math.log2(math.e)
W_GROUP = H_A * DV_A
N_GROUPS_QKV = 6
TOKEN_MINOR_GROUPS = (1, 4, 5)
MIX = 2 * W_GROUP

LANES = 128
SUBLANES = 8
HALF = LANES // 2
VMEM_LIMIT = 56 * 1024 * 1024

ALIBI_LOG2_STEP = 8 // H_A
assert 8 % H_A == 0

TQ = 512
TK_DIFF = 512
TK_SB = 256
ROW_CHUNK = 128
STREAMS = 4
Q_PAD = SUBLANES
ROWS_S = 2 * H_A * Q_PAD
assert ROWS_S == H_B * Q_PAD
PAGES_PER_STEP = 16


def _pow2_neg(e):
    return lax.bitcast_convert_type((127 - e) << 23, F32)


def _rms(x, g):
    return x * lax.rsqrt(jnp.mean(x * x, axis=-1, keepdims=True) + EPS) * g


def _nt_dot(a, b):
    return lax.dot_general(a, b, (((1,), (1,)), ((), ())), preferred_element_type=F32)


def _dot(a, b):
    return jnp.dot(a, b, preferred_element_type=F32)


def _lambda(lam_ref, lam_init):
    t1 = jnp.sum(lam_ref[0] * lam_ref[1], axis=-1, keepdims=True)
    t2 = jnp.sum(lam_ref[2] * lam_ref[3], axis=-1, keepdims=True)
    return jnp.exp(t1) - jnp.exp(t2) + lam_init


def _softplus(z):
    return jnp.maximum(z, 0.0) + jnp.log(1.0 + jnp.exp2(jnp.abs(z) * (-LOG2_E)))


def _strict_lower_ones(n):
    r = lax.broadcasted_iota(jnp.int32, (n, n), 0)
    c = lax.broadcasted_iota(jnp.int32, (n, n), 1)
    return jnp.where(r > c, 1.0, 0.0).astype(BF16)


def _proj_kernel(x_ref, g_ref, w_ref, wt_ref, zb_ref, kta_ref, va_ref, ktb_ref, vtb_ref, gate_ref):
    h = _rms(x_ref[...], g_ref[...]).astype(BF16)
    tm = h.shape[0]

    def cols(i):
        return _dot(h, w_ref[:, i * W_GROUP:(i + 1) * W_GROUP])

    for i in range(N_GROUPS_QKV):
        z = cols(i)
        if i == 2:
            for hh in range(H_A):
                va_ref[pl.ds(hh, tm, stride=H_A), :] = z[:, hh * DV_A:(hh + 1) * DV_A]
        if i in (0, 3):
            z = z * (DH_A ** -0.5)
        zb_ref[:, i * W_GROUP:(i + 1) * W_GROUP] = z.astype(BF16)
    for slot, ref in enumerate((kta_ref, ktb_ref, vtb_ref)):
        ref[...] = _nt_dot(wt_ref[slot * W_GROUP:(slot + 1) * W_GROUP, :], h)
    for i in range(2):
        gate_ref[:, i * W_GROUP:(i + 1) * W_GROUP] = cols(N_GROUPS_QKV + i)


def _proj(x, g, w_bf16, wt_bf16, tm, rows_per_batch):
    n, d = x.shape
    d_in = w_bf16.shape[1]
    assert n % rows_per_batch == 0 and rows_per_batch % tm == 0
    assert d_in == (N_GROUPS_QKV + 2) * W_GROUP
    tiles_per_batch = rows_per_batch // tm
    row = lambda i: (i, 0)
    fixed = lambda i: (0, 0)
    tok_minor = lambda i: (i // tiles_per_batch, 0, i % tiles_per_batch)
    kt = jax.ShapeDtypeStruct((n // rows_per_batch, W_GROUP, rows_per_batch), F32)
    kt_spec = pl.BlockSpec((None, W_GROUP, tm), tok_minor)
    return pl.pallas_call(
        _proj_kernel,
        grid=(n // tm,),
        in_specs=[pl.BlockSpec((tm, d), row), pl.BlockSpec((1, d), fixed),
                  pl.BlockSpec((d, d_in), fixed), pl.BlockSpec(wt_bf16.shape, fixed)],
        out_specs=[pl.BlockSpec((tm, N_GROUPS_QKV * W_GROUP), row), kt_spec,
                   pl.BlockSpec((tm * H_A, DV_A), row), kt_spec, kt_spec,
                   pl.BlockSpec((tm, MIX), row)],
        out_shape=[jax.ShapeDtypeStruct((n, N_GROUPS_QKV * W_GROUP), BF16), kt,
                   jax.ShapeDtypeStruct((n * H_A, DV_A), F32), kt, kt,
                   jax.ShapeDtypeStruct((n, MIX), F32)],
        compiler_params=pltpu.CompilerParams(dimension_semantics=("parallel",),
                                             vmem_limit_bytes=VMEM_LIMIT),
        name="proj",
    )(x, g.reshape(1, d), w_bf16, wt_bf16)


def _stack_halves(x):
    lane = lax.broadcasted_iota(jnp.int32, x.shape, 1)
    zero = jnp.zeros_like(x)
    return jnp.concatenate([jnp.where(lane < HALF, x, zero), jnp.where(lane >= HALF, x, zero)], axis=0)


def _row_chunks():
    return [(r0, r0 + ROW_CHUNK) for r0 in range(0, 2 * TQ, ROW_CHUNK)]


def _causal_iotas(r0, tk):
    r = (lax.broadcasted_iota(jnp.int32, (ROW_CHUNK, tk), 0) + r0) & (TQ - 1)
    c = lax.broadcasted_iota(jnp.int32, (ROW_CHUNK, tk), 1)
    return r, c


def _lane_replicated(x):
    return jnp.broadcast_to(x, (x.shape[0], LANES))


def _lane_tiled(x, width):
    return jnp.concatenate([x] * (width // LANES), axis=1)


def _stream_lanes():
    return [slice(s * LANES, (s + 1) * LANES) for s in range(STREAMS)]


def _diff_prompt_kernel(q_ref, k_ref, v_ref, lam_ref, g_ref, o_ref, *, lam_init):
    tk = TK_DIFF
    n_sub = TQ // tk
    qi = pl.program_id(2)
    lanes = _stream_lanes()
    chunks = _row_chunks()
    qs = [_stack_halves(q_ref[:, ln]) for ln in lanes]
    heads = [pl.program_id(1) * STREAMS + s for s in range(STREAMS)]
    slope = [_pow2_neg(jnp.full((1, tk), ALIBI_LOG2_STEP, jnp.int32) * (h + 1)) for h in heads]
    col = lax.broadcasted_iota(jnp.int32, (1, tk), 1).astype(F32)
    ones = jnp.ones((tk, LANES), BF16)

    def block(kj, carry, key_offset, diag_sub=None):
        start = pl.multiple_of(kj * tk, tk)
        s_all = [_nt_dot(qs[s], k_ref[pl.ds(start, tk), ln]) for s, ln in enumerate(lanes)]
        vs = [jnp.concatenate([v_ref[pl.ds(start, tk), ln], ones], axis=1) for ln in lanes]
        bias = [sl * (col + key_offset) for sl in slope]

        def scores(s, r0, r1):
            x = s_all[s][r0:r1] + bias[s]
            if diag_sub is not None:
                r, c = _causal_iotas(r0, tk)
                x = jnp.where(c + diag_sub * tk <= r, x, NEG_INF)
            return x

        m_new = [jnp.concatenate(
            [jnp.maximum(carry[s][0][r0:r1],
                         _lane_replicated(jnp.max(scores(s, r0, r1), axis=-1, keepdims=True)))
             for r0, r1 in chunks], axis=0) for s in range(STREAMS)]
        p = [jnp.concatenate(
            [jnp.exp(scores(s, r0, r1) - _lane_tiled(m_new[s][r0:r1], tk)).astype(BF16)
             for r0, r1 in chunks], axis=0) for s in range(STREAMS)]
        out = []
        for s in range(STREAMS):
            m, acc = carry[s]
            alpha = jnp.exp(m - m_new[s])
            out.append((m_new[s], _lane_tiled(alpha, 2 * LANES) * acc + _dot(p[s], vs[s])))
        return tuple(out)

    carry = tuple((jnp.full((2 * TQ, LANES), NEG_INF, F32), jnp.zeros((2 * TQ, 2 * LANES), F32))
                  for _ in range(STREAMS))
    carry = lax.fori_loop(
        0, qi * n_sub, lambda kj, c: block(kj, c, (kj * tk - qi * TQ).astype(F32)), carry)
    for sub in range(n_sub):
        carry = block(qi * n_sub + sub, carry, float(sub * tk), diag_sub=sub)
    lam = _lambda(lam_ref, lam_init)
    for ln, (_, acc) in zip(lanes, carry):
        o = acc[:, :LANES] / acc[:, LANES:]
        d = o[:TQ] - lam * o[TQ:]
        o_ref[:, ln] = _rms(d, g_ref[...]) * (1.0 - lam_init)


def _sb_prompt_kernel(q_ref, k_ref, v_ref, o_ref):
    tk = TK_SB
    n_sub = TQ // tk
    qi = pl.program_id(2)
    lanes = _stream_lanes()
    chunks = _row_chunks()
    half = len(chunks) // 2
    qs = [_stack_halves(q_ref[:, ln]) for ln in lanes]
    tri = _strict_lower_ones(tk)

    def block(kj, carry, diag_sub=None):
        start = pl.multiple_of(kj * tk, tk)
        z_all = [_nt_dot(qs[s], k_ref[pl.ds(start, tk), ln]) for s, ln in enumerate(lanes)]
        vs = [_stack_halves(v_ref[pl.ds(start, tk), ln]) for ln in lanes]
        masks = [None] * len(chunks)
        if diag_sub is not None:
            for i, (r0, _) in enumerate(chunks):
                r, c = _causal_iotas(r0, tk)
                masks[i] = c + diag_sub * tk < r
        log_beta, drops, sums = [], [], []
        for s in range(STREAMS):
            lb_s, drop_s, sum_s = [], [], []
            for (r0, r1), mask in zip(chunks, masks):
                z = z_all[s][r0:r1]
                drop = _softplus(z)
                lb_s.append(z - drop)
                if mask is not None:
                    drop = jnp.where(mask, drop, 0.0)
                drop_s.append(drop.astype(BF16))
                sum_s.append(_lane_replicated(jnp.sum(drop, axis=-1, keepdims=True)))
            log_beta.append(lb_s)
            drops.append(jnp.concatenate(drop_s, axis=0))
            sums.append(jnp.concatenate(sum_s, axis=0))
        after = [_dot(drops[s], tri) for s in range(STREAMS)]
        out = []
        for s in range(STREAMS):
            run, acc = carry[s]
            w_out = []
            for (r0, r1), mask, lb in zip(chunks, masks, log_beta[s]):
                w = jnp.exp(lb - after[s][r0:r1] - _lane_tiled(run[r0:r1], tk))
                if mask is not None:
                    w = jnp.where(mask, w, 0.0)
                w_out.append(w.astype(BF16))
            wb = jnp.concatenate([jnp.concatenate(w_out[:half], axis=0),
                                  jnp.concatenate(w_out[half:], axis=0)], axis=1)
            out.append((run + sums[s], acc + _dot(wb, vs[s])))
        return tuple(out)

    carry = tuple((jnp.zeros((2 * TQ, LANES), F32), jnp.zeros((TQ, LANES), F32))
                  for _ in range(STREAMS))
    for sub in reversed(range(n_sub)):
        carry = block(qi * n_sub + sub, carry, diag_sub=sub)
    carry = lax.fori_loop(0, qi * n_sub, lambda i, c: block(qi * n_sub - 1 - i, c), carry)
    for ln, (_, acc) in zip(lanes, carry):
        o_ref[:, ln] = acc


def _prompt_attention(body, zb, extra, q_blk, k_blk, v_blk, name):
    b, t, _ = zb.shape
    assert t % TQ == 0 and TQ % TK_DIFF == 0 and TQ % TK_SB == 0
    width = STREAMS * LANES
    assert W_GROUP % width == 0
    n_grp = W_GROUP // width
    extra_specs = [pl.BlockSpec(e.shape, lambda bi, g, qi, nd=e.ndim: (0,) * nd) for e in extra]
    return pl.pallas_call(
        body,
        grid=(b, n_grp, t // TQ),
        in_specs=[pl.BlockSpec((None, TQ, width), lambda bi, g, qi: (bi, qi, q_blk * n_grp + g)),
                  pl.BlockSpec((None, t, width), lambda bi, g, qi: (bi, 0, k_blk * n_grp + g)),
                  pl.BlockSpec((None, t, width), lambda bi, g, qi: (bi, 0, v_blk * n_grp + g))]
        + extra_specs,
        out_specs=pl.BlockSpec((None, TQ, width), lambda bi, g, qi: (bi, qi, g)),
        out_shape=jax.ShapeDtypeStruct((b, t, W_GROUP), F32),
        compiler_params=pltpu.CompilerParams(
            dimension_semantics=("parallel", "parallel", "parallel"),
            vmem_limit_bytes=VMEM_LIMIT),
        name=name,
    )(zb, zb, zb, *extra)


def _sample_kernel(pt_ref, qa_ref, qb_ref, kna_ref, vna_ref, knb_ref, vnb_ref, *rest,
                   past, page, lam_init):
    del pt_ref
    n_cached = 4 * PAGES_PER_STEP
    pools = [rest[i * PAGES_PER_STEP:(i + 1) * PAGES_PER_STEP] for i in range(4)]
    lam_ref, g_ref, oa_ref, ob_ref, m_sc, l_sc, acca_sc, run_sc, accb_sc = rest[n_cached:]
    j = pl.program_id(1)
    n_pages = past // page
    rows_h = 2 * Q_PAD
    r = lax.broadcasted_iota(jnp.int32, (ROWS_S, page), 0)
    c = lax.broadcasted_iota(jnp.int32, (ROWS_S, page), 1)
    col = c.astype(F32)
    q_of_row = r & (Q_PAD - 1)
    slope = _pow2_neg(ALIBI_LOG2_STEP * ((r // rows_h) + 1))
    tri = _strict_lower_ones(page)

    def attend(blocks, masked):
        qa = qa_ref[...]
        qb = qb_ref[...]
        scores = [_dot(qa, blk[0][...].astype(BF16)) for blk in blocks]
        logits = [_dot(qb, blk[2][...].astype(BF16)) for blk in blocks]

        for i, blk in enumerate(blocks):
            scores[i] = scores[i] + slope * (col + blk[4])
            if masked:
                scores[i] = jnp.where(c <= q_of_row, scores[i], NEG_INF)
        m = m_sc[...]
        m_new = jnp.maximum(m, jnp.max(functools.reduce(jnp.maximum, scores), axis=-1, keepdims=True))
        alpha = jnp.exp(m - m_new)
        probs = [jnp.exp(s - m_new) for s in scores]
        m_sc[...] = m_new
        l_sc[...] = alpha * l_sc[...] + jnp.sum(functools.reduce(jnp.add, probs), axis=-1,
                                                keepdims=True)
        probs = [p.astype(BF16) for p in probs]

        mask = (c < q_of_row) if masked else None
        log_beta, drops, sums = [], [], []
        for z in logits:
            drop = _softplus(z)
            log_beta.append(z - drop)
            if masked:
                drop = jnp.where(mask, drop, 0.0)
            drops.append(drop.astype(BF16))
            sums.append(jnp.sum(drop, axis=-1, keepdims=True))
        after = [_dot(d, tri) for d in drops]

        for h in range(H_A):
            rows = slice(h * rows_h, (h + 1) * rows_h)
            upd = None
            for blk, p in zip(blocks, probs):
                v_h = blk[1][pl.ds(h, page, stride=H_A), :].astype(BF16)
                d = _dot(p[rows], v_h)
                upd = d if upd is None else upd + d
            acca_sc[rows, :] = alpha[rows] * acca_sc[rows, :] + upd

        run = run_sc[...]
        weights = []
        for lb, aft, rs in zip(log_beta, after, sums):
            w = jnp.exp(lb - aft - run)
            if masked:
                w = jnp.where(mask, w, 0.0)
            weights.append(w.astype(BF16))
            run = run + rs
        run_sc[...] = run
        upd = None
        for blk, w in zip(blocks, weights):
            d = _nt_dot(w, blk[3][...].astype(BF16))
            upd = d if upd is None else upd + d
        accb_sc[...] = accb_sc[...] + upd

    @pl.when(j == 0)
    def _():
        m_sc[...] = jnp.full(m_sc.shape, NEG_INF, F32)
        l_sc[...] = jnp.zeros(l_sc.shape, F32)
        acca_sc[...] = jnp.zeros(acca_sc.shape, F32)
        run_sc[...] = jnp.zeros(run_sc.shape, F32)
        accb_sc[...] = jnp.zeros(accb_sc.shape, F32)
        attend([(kna_ref, vna_ref, knb_ref, vnb_ref, 0.0)], True)

    @pl.when(j > 0)
    def _():
        newest = n_pages - 1 - (j - 1) * PAGES_PER_STEP
        attend([(pools[0][i], pools[1][i], pools[2][i], pools[3][i],
                 ((newest - i) * page - past).astype(F32)) for i in range(PAGES_PER_STEP)], False)

    @pl.when(j == n_pages // PAGES_PER_STEP)
    def _():
        o = acca_sc[...] / l_sc[...]
        lam = _lambda(lam_ref, lam_init)
        for h in range(H_A):
            d = (o[h * rows_h:h * rows_h + Q_PAD]
                 - lam * o[h * rows_h + Q_PAD:(h + 1) * rows_h])
            oa_ref[:, h * DV_A:(h + 1) * DV_A] = _rms(d, g_ref[...]) * (1.0 - lam_init)
        acc = accb_sc[...]
        lane_head = lax.broadcasted_iota(jnp.int32, (Q_PAD, W_GROUP), 1) // DH_B
        out = jnp.zeros((Q_PAD, W_GROUP), F32)
        for h in range(H_B):
            out = out + jnp.where(lane_head == h, acc[h * Q_PAD:(h + 1) * Q_PAD], 0.0)
        ob_ref[...] = out


def _sample_attention(page_table, qbd_a, qbd_b, kn_a, vn_a, kn_b, vn_b,
                      pool_arrays, lam_pack, g_subln, pool_base, lam_init):
    db, n_pages = page_table.shape
    page = pool_arrays[0].shape[2]
    past = n_pages * page
    assert n_pages % PAGES_PER_STEP == 0

    per_b = lambda b, j, pt: (b, 0, 0)

    def paged(i):
        def index(b, j, pt):
            newest = n_pages - 1 - (jnp.maximum(j, 1) - 1) * PAGES_PER_STEP
            return (pool_base + pt[b, newest - i], 0, 0)
        return pl.BlockSpec((None, W_GROUP, page), index)

    new_spec = pl.BlockSpec((None, W_GROUP, page), per_b)
    q_spec = pl.BlockSpec((None, ROWS_S, W_GROUP), per_b)
    out_spec = pl.BlockSpec((None, Q_PAD, W_GROUP), per_b)
    out = jax.ShapeDtypeStruct((db, Q_PAD, W_GROUP), F32)
    cache_specs = [paged(i) for _ in pool_arrays for i in range(PAGES_PER_STEP)]
    cache_args = [a for a in pool_arrays for _ in range(PAGES_PER_STEP)]
    return pl.pallas_call(
        functools.partial(_sample_kernel, past=past, page=page, lam_init=lam_init),
        grid_spec=pltpu.PrefetchScalarGridSpec(
            num_scalar_prefetch=1,
            grid=(db, n_pages // PAGES_PER_STEP + 1),
            in_specs=[q_spec, q_spec, new_spec, new_spec, new_spec, new_spec] + cache_specs
            + [pl.BlockSpec(lam_pack.shape, lambda b, j, pt: (0, 0, 0)),
               pl.BlockSpec(g_subln.shape, lambda b, j, pt: (0, 0))],
            out_specs=[out_spec, out_spec],
            scratch_shapes=[pltpu.VMEM((ROWS_S, 1), F32), pltpu.VMEM((ROWS_S, 1), F32),
                            pltpu.VMEM((ROWS_S, DV_A), F32), pltpu.VMEM((ROWS_S, 1), F32),
                            pltpu.VMEM((ROWS_S, W_GROUP), F32)]),
        out_shape=[out, out],
        compiler_params=pltpu.CompilerParams(dimension_semantics=("parallel", "arbitrary"),
                                             vmem_limit_bytes=VMEM_LIMIT),
        name="sample_attention",
    )(page_table, qbd_a, qbd_b, kn_a, vn_a, kn_b, vn_b, *cache_args, lam_pack, g_subln)


def _sigmoid(x):
    return 1.0 / (1.0 + jnp.exp(-x))


def _merge_kernel(x_ref, oa_ref, ob_ref, gate_ref, p_ref, wo_ref, gp_ref, wg_ref, wp_ref, gf_ref,
                  y_ref, *, final):
    gate = gate_ref[...]
    sg = gate * _sigmoid(gate)
    ua = (oa_ref[...] * sg[:, :W_GROUP]).astype(BF16)
    ub = (ob_ref[...] * sg[:, W_GROUP:]).astype(BF16)
    x1 = x_ref[...] + _dot(ua, wo_ref[:W_GROUP, :]) + _dot(ub, wo_ref[W_GROUP:, :])
    hn = _rms(x1, gp_ref[...]).astype(BF16)
    g = _sigmoid(_dot(hn, wg_ref[...]))
    pe = _dot(p_ref[...].astype(BF16), wp_ref[...])
    x2 = x1 + g * pe
    y_ref[...] = _rms(x2, gf_ref[...]) if final else x2


def _merge(x, o_a, o_b, gate, p, w_out, g_ple, w_gate, w_proj, g_final, tm, final):
    n, d = x.shape
    assert n % tm == 0
    row = lambda i: (i, 0)
    fixed = lambda i: (0, 0)
    full = lambda a: pl.BlockSpec(a.shape, fixed)
    g_ple = g_ple.reshape(1, d)
    g_final = g_final.reshape(1, d)
    return pl.pallas_call(
        functools.partial(_merge_kernel, final=final),
        grid=(n // tm,),
        in_specs=[pl.BlockSpec((tm, d), row), pl.BlockSpec((tm, W_GROUP), row),
                  pl.BlockSpec((tm, W_GROUP), row), pl.BlockSpec((tm, MIX), row),
                  pl.BlockSpec((tm, p.shape[1]), row), full(w_out), full(g_ple), full(w_gate),
                  full(w_proj), full(g_final)],
        out_specs=pl.BlockSpec((tm, d), row),
        out_shape=jax.ShapeDtypeStruct((n, d), F32),
        compiler_params=pltpu.CompilerParams(dimension_semantics=("parallel",),
                                             vmem_limit_bytes=VMEM_LIMIT),
        name="merge",
    )(x, o_a, o_b, gate, p, w_out, g_ple, w_gate, w_proj, g_final)


def _block_diag_queries(q, group_width):
    n_groups = W_GROUP // group_width
    lane_group = jnp.arange(W_GROUP, dtype=jnp.int32) // group_width
    mask = (lane_group[None, :] == jnp.arange(n_groups, dtype=jnp.int32)[:, None]).astype(q.dtype)
    out = q[:, None, :, :] * mask[None, :, None, :]
    return out.reshape(q.shape[0], n_groups * Q_PAD, W_GROUP)


def _tokens_major(kt, lead, tail):
    nb, _, rows = kt.shape
    a = jnp.moveaxis(kt.reshape(nb, *tail, rows), -1, 1)
    return a.reshape(*lead, *tail)


def kernel(x_prompt, x_sample, cache_k_diff, cache_v_diff, cache_k_sb, cache_v_sb, page_table,
           p_prompt, p_sample, g_norm, w_in, lambda_q1, lambda_k1, lambda_q2, lambda_k2,
           g_subln, w_out, g_ple, w_ple_gate, w_ple_proj, g_final):
    b, t, d = x_prompt.shape
    db, ds, _ = x_sample.shape
    depth, n_pool, page = cache_k_diff.shape[:3]
    assert ds <= Q_PAD and page * H_A == W_GROUP and DV_A == page

    xp = x_prompt.reshape(b * t, d)
    xs = x_sample.reshape(db * ds, d)
    n_all = depth * n_pool
    pools = [cache_k_diff.transpose(0, 1, 3, 4, 5, 2).reshape(n_all, W_GROUP, page),
             cache_v_diff.reshape(n_all, page * H_A, DV_A),
             cache_k_sb.transpose(0, 1, 3, 4, 2).reshape(n_all, W_GROUP, page),
             cache_v_sb.transpose(0, 1, 3, 4, 2).reshape(n_all, W_GROUP, page)]

    kd, vd, ksb = (H_A, 2, DH_A), (H_A, DV_A), (H_B, DH_B)
    new_p = [[] for _ in range(4)]
    new_s = [[] for _ in range(4)]
    for l in range(depth):
        lam_init = 0.8 - 0.6 * math.exp(-0.3 * l)
        final = l == depth - 1
        w_in_l = w_in[l].astype(BF16)
        wt_in_l = jnp.concatenate(
            [w_in_l[:, i * W_GROUP:(i + 1) * W_GROUP].T for i in TOKEN_MINOR_GROUPS], axis=0)
        w_out_l = w_out[l].astype(BF16)
        w_gate_l = w_ple_gate[l].astype(BF16)
        w_proj_l = w_ple_proj[l].astype(BF16)
        lam_pack = jnp.stack([lambda_q1[l], lambda_k1[l], lambda_q2[l], lambda_k2[l]])[:, None, :]
        g_sub = g_subln[l].reshape(1, DV_A)

        zb, kta, va, ktb, vtb, gate = _proj(xp, g_norm[l], w_in_l, wt_in_l, 512, t)
        zb3 = zb.reshape(b, t, N_GROUPS_QKV * W_GROUP)
        o_a = _prompt_attention(functools.partial(_diff_prompt_kernel, lam_init=lam_init),
                                zb3, [lam_pack, g_sub], 0, 1, 2, "diff_prompt")
        o_b = _prompt_attention(_sb_prompt_kernel, zb3, [], 3, 4, 5, "sb_prompt")
        xp = _merge(xp, o_a.reshape(b * t, W_GROUP), o_b.reshape(b * t, W_GROUP), gate,
                    p_prompt[l].reshape(b * t, -1), w_out_l, g_ple[l], w_gate_l, w_proj_l,
                    g_final, 512, final)
        new_p[0].append(_tokens_major(kta, (b, t), kd))
        new_p[1].append(va.reshape(b, t, *vd))
        new_p[2].append(_tokens_major(ktb, (b, t), ksb))
        new_p[3].append(_tokens_major(vtb, (b, t), ksb))

        zb, kta, va, ktb, vtb, gate = _proj(xs, g_norm[l], w_in_l, wt_in_l, db * ds, db * ds)
        zs = zb.reshape(db, ds, N_GROUPS_QKV, W_GROUP)
        q_pad = lambda i: jnp.pad(zs[:, :, i], ((0, 0), (0, Q_PAD - ds), (0, 0)))
        per_b = lambda kt: jnp.pad(kt.reshape(W_GROUP, db, ds).transpose(1, 0, 2),
                                   ((0, 0), (0, 0), (0, page - ds)))
        vn_a = jnp.pad(va.reshape(db, ds, H_A, DV_A), ((0, 0), (0, page - ds), (0, 0), (0, 0)))
        so_a, so_b = _sample_attention(
            page_table, _block_diag_queries(q_pad(0), DH_A), _block_diag_queries(q_pad(3), DH_B),
            per_b(kta), vn_a.reshape(db, page * H_A, DV_A), per_b(ktb), per_b(vtb),
            pools, lam_pack, g_sub, l * n_pool, lam_init)
        xs = _merge(xs, so_a[:, :ds].reshape(db * ds, W_GROUP), so_b[:, :ds].reshape(db * ds, W_GROUP),
                    gate, p_sample[l].reshape(db * ds, -1), w_out_l, g_ple[l], w_gate_l, w_proj_l,
                    g_final, db * ds, final)
        new_s[0].append(_tokens_major(kta, (db, ds), kd))
        new_s[1].append(va.reshape(db, ds, *vd))
        new_s[2].append(_tokens_major(ktb, (db, ds), ksb))
        new_s[3].append(_tokens_major(vtb, (db, ds), ksb))

    stacked = lambda parts: jnp.stack(parts) if depth > 1 else parts[0][None]
    return (xp.reshape(b, t, d), xs.reshape(db, ds, d),
            *[stacked(parts) for parts in new_p], *[stacked(parts) for parts in new_s])
```

```python
import functools
import math

import jax
import jax.numpy as jnp
from jax import lax
from jax.experimental import pallas as pl
from jax.experimental.pallas import tpu as pltpu

F32 = jnp.float32
BF16 = jnp.bfloat16

H_A = 4
DH_A = 64
DV_A = 2 * DH_A
H_B = 8
DH_B = 64
EPS = 1e-6
NEG_INF = -1e30
LOG2_E = math.log2(math.e)
W_GROUP = H_A * DV_A
N_GROUPS_QKV = 6
TOKEN_MINOR_GROUPS = (1, 4, 5)
MIX = 2 * W_GROUP

LANES = 128
SUBLANES = 8
HALF = LANES // 2
VMEM_LIMIT = 56 * 1024 * 1024

ALIBI_LOG2_STEP = 8 // H_A
assert 8 % H_A == 0

TQ = 512
TK_DIFF = 512
TK_SB = 256
ROW_CHUNK = 128
STREAMS = 4
Q_PAD = SUBLANES
ROWS_S = 2 * H_A * Q_PAD
assert ROWS_S == H_B * Q_PAD
PAGES_PER_STEP = 16


def _pow2_neg(e):
    return lax.bitcast_convert_type((127 - e) << 23, F32)


def _rms(x, g):
    return x * lax.rsqrt(jnp.mean(x * x, axis=-1, keepdims=True) + EPS) * g


def _nt_dot(a, b):
    return lax.dot_general(a, b, (((1,), (1,)), ((), ())), preferred_element_type=F32)


def _dot(a, b):
    return jnp.dot(a, b, preferred_element_type=F32)


def _lambda(lam_ref, lam_init):
    t1 = jnp.sum(lam_ref[0] * lam_ref[1], axis=-1, keepdims=True)
    t2 = jnp.sum(lam_ref[2] * lam_ref[3], axis=-1, keepdims=True)
    return jnp.exp(t1) - jnp.exp(t2) + lam_init


def _softplus(z):
    return jnp.maximum(z, 0.0) + jnp.log(1.0 + jnp.exp2(jnp.abs(z) * (-LOG2_E)))


def _strict_lower_ones(n):
    r = lax.broadcasted_iota(jnp.int32, (n, n), 0)
    c = lax.broadcasted_iota(jnp.int32, (n, n), 1)
    return jnp.where(r > c, 1.0, 0.0).astype(BF16)


def _proj_kernel(x_ref, g_ref, w_ref, zb_ref, kta_ref, va_ref, ktb_ref, vtb_ref, gate_ref):
    h = _rms(x_ref[...], g_ref[...]).astype(BF16)
    tm = h.shape[0]

    def cols(i):
        return _dot(h, w_ref[:, i * W_GROUP:(i + 1) * W_GROUP])

    token_minor = dict(zip(TOKEN_MINOR_GROUPS, (kta_ref, ktb_ref, vtb_ref)))
    for i in range(N_GROUPS_QKV):
        z = cols(i)
        if i == 2:
            for hh in range(H_A):
                va_ref[pl.ds(hh, tm, stride=H_A), :] = z[:, hh * DV_A:(hh + 1) * DV_A]
        if i in token_minor:
            token_minor[i][...] = z.T
        if i in (0, 3):
            z = z * (DH_A ** -0.5)
        zb_ref[:, i * W_GROUP:(i + 1) * W_GROUP] = z.astype(BF16)
    for i in range(2):
        gate_ref[:, i * W_GROUP:(i + 1) * W_GROUP] = cols(N_GROUPS_QKV + i)


def _proj(x, g, w_bf16, tm, rows_per_batch):
    n, d = x.shape
    d_in = w_bf16.shape[1]
    assert n % rows_per_batch == 0 and rows_per_batch % tm == 0
    assert d_in == (N_GROUPS_QKV + 2) * W_GROUP
    tiles_per_batch = rows_per_batch // tm
    row = lambda i: (i, 0)
    fixed = lambda i: (0, 0)
    tok_minor = lambda i: (i // tiles_per_batch, 0, i % tiles_per_batch)
    kt = jax.ShapeDtypeStruct((n // rows_per_batch, W_GROUP, rows_per_batch), F32)
    kt_spec = pl.BlockSpec((None, W_GROUP, tm), tok_minor)
    return pl.pallas_call(
        _proj_kernel,
        grid=(n // tm,),
        in_specs=[pl.BlockSpec((tm, d), row), pl.BlockSpec((1, d), fixed),
                  pl.BlockSpec((d, d_in), fixed)],
        out_specs=[pl.BlockSpec((tm, N_GROUPS_QKV * W_GROUP), row), kt_spec,
                   pl.BlockSpec((tm * H_A, DV_A), row), kt_spec, kt_spec,
                   pl.BlockSpec((tm, MIX), row)],
        out_shape=[jax.ShapeDtypeStruct((n, N_GROUPS_QKV * W_GROUP), BF16), kt,
                   jax.ShapeDtypeStruct((n * H_A, DV_A), F32), kt, kt,
                   jax.ShapeDtypeStruct((n, MIX), F32)],
        compiler_params=pltpu.CompilerParams(dimension_semantics=("parallel",),
                                             vmem_limit_bytes=VMEM_LIMIT),
        name="proj",
    )(x, g.reshape(1, d), w_bf16)


def _stack_halves(x):
    lane = lax.broadcasted_iota(jnp.int32, x.shape, 1)
    zero = jnp.zeros_like(x)
    return jnp.concatenate([jnp.where(lane < HALF, x, zero), jnp.where(lane >= HALF, x, zero)], axis=0)


def _row_chunks():
    return [(r0, r0 + ROW_CHUNK) for r0 in range(0, 2 * TQ, ROW_CHUNK)]


def _causal_iotas(r0, tk):
    r = (lax.broadcasted_iota(jnp.int32, (ROW_CHUNK, tk), 0) + r0) & (TQ - 1)
    c = lax.broadcasted_iota(jnp.int32, (ROW_CHUNK, tk), 1)
    return r, c


def _lane_replicated(x):
    return jnp.broadcast_to(x, (x.shape[0], LANES))


def _lane_tiled(x, width):
    return jnp.concatenate([x] * (width // LANES), axis=1)


def _stream_lanes():
    return [slice(s * LANES, (s + 1) * LANES) for s in range(STREAMS)]


def _diff_prompt_kernel(q_ref, k_ref, v_ref, lam_ref, g_ref, o_ref, m_sc, acc_sc, *, lam_init):
    tk = TK_DIFF
    n_sub = TQ // tk
    qi = pl.program_id(2)
    lanes = _stream_lanes()
    chunks = _row_chunks()
    qs = [_stack_halves(q_ref[:, ln]) for ln in lanes]
    heads = [pl.program_id(1) * STREAMS + s for s in range(STREAMS)]
    slope = [_pow2_neg(jnp.full((1, tk), ALIBI_LOG2_STEP, jnp.int32) * (h + 1)) for h in heads]
    col = lax.broadcasted_iota(jnp.int32, (1, tk), 1).astype(F32)
    ones = jnp.ones((tk, LANES), BF16)

    m_sc[...] = jnp.full(m_sc.shape, NEG_INF, F32)
    acc_sc[...] = jnp.zeros(acc_sc.shape, F32)

    def block(kj, key_offset, diag_sub=None):
        start = pl.multiple_of(kj * tk, tk)
        s_all = [_nt_dot(qs[s], k_ref[pl.ds(start, tk), ln]) for s, ln in enumerate(lanes)]
        vs = [jnp.concatenate([v_ref[pl.ds(start, tk), ln], ones], axis=1) for ln in lanes]
        bias = [sl * (col + key_offset) for sl in slope]

        def scores(s, r0, r1):
            x = s_all[s][r0:r1] + bias[s]
            if diag_sub is not None:
                r, c = _causal_iotas(r0, tk)
                x = jnp.where(c + diag_sub * tk <= r, x, NEG_INF)
            return x

        m_new = [jnp.concatenate(
            [jnp.maximum(m_sc[s, r0:r1, :],
                         _lane_replicated(jnp.max(scores(s, r0, r1), axis=-1, keepdims=True)))
             for r0, r1 in chunks], axis=0) for s in range(STREAMS)]
        p = [jnp.concatenate(
            [jnp.exp(scores(s, r0, r1) - _lane_tiled(m_new[s][r0:r1], tk)).astype(BF16)
             for r0, r1 in chunks], axis=0) for s in range(STREAMS)]
        for s in range(STREAMS):
            alpha = jnp.exp(m_sc[s] - m_new[s])
            acc_sc[s] = _lane_tiled(alpha, 2 * LANES) * acc_sc[s] + _dot(p[s], vs[s])
            m_sc[s] = m_new[s]

    def full_block(kj, _):
        block(kj, (kj * tk - qi * TQ).astype(F32))
        return 0

    lax.fori_loop(0, qi * n_sub, full_block, 0)
    for sub in range(n_sub):
        block(qi * n_sub + sub, float(sub * tk), diag_sub=sub)
    lam = _lambda(lam_ref, lam_init)
    for s, ln in enumerate(lanes):
        acc = acc_sc[s]
        o = acc[:, :LANES] / acc[:, LANES:]
        d = o[:TQ] - lam * o[TQ:]
        o_ref[:, ln] = _rms(d, g_ref[...]) * (1.0 - lam_init)


def _sb_prompt_kernel(q_ref, k_ref, v_ref, o_ref, run_sc, acc_sc):
    tk = TK_SB
    n_sub = TQ // tk
    qi = pl.program_id(2)
    lanes = _stream_lanes()
    chunks = _row_chunks()
    half = len(chunks) // 2
    qs = [_stack_halves(q_ref[:, ln]) for ln in lanes]
    tri = _strict_lower_ones(tk)
    run_sc[...] = jnp.zeros(run_sc.shape, F32)
    acc_sc[...] = jnp.zeros(acc_sc.shape, F32)

    def block(kj, diag_sub=None):
        start = pl.multiple_of(kj * tk, tk)
        z_all = [_nt_dot(qs[s], k_ref[pl.ds(start, tk), ln]) for s, ln in enumerate(lanes)]
        vs = [_stack_halves(v_ref[pl.ds(start, tk), ln]) for ln in lanes]
        masks = [None] * len(chunks)
        if diag_sub is not None:
            for i, (r0, _) in enumerate(chunks):
                r, c = _causal_iotas(r0, tk)
                masks[i] = c + diag_sub * tk < r
        log_beta, drops, sums = [], [], []
        for s in range(STREAMS):
            lb_s, drop_s, sum_s = [], [], []
            for (r0, r1), mask in zip(chunks, masks):
                z = z_all[s][r0:r1]
                drop = _softplus(z)
                lb_s.append(z - drop)
                if mask is not None:
                    drop = jnp.where(mask, drop, 0.0)
                drop_s.append(drop.astype(BF16))
                sum_s.append(_lane_replicated(jnp.sum(drop, axis=-1, keepdims=True)))
            log_beta.append(lb_s)
            drops.append(jnp.concatenate(drop_s, axis=0))
            sums.append(jnp.concatenate(sum_s, axis=0))
        after = [_dot(drops[s], tri) for s in range(STREAMS)]
        for s in range(STREAMS):
            w_out = []
            for (r0, r1), mask, lb in zip(chunks, masks, log_beta[s]):
                w = jnp.exp(lb - after[s][r0:r1] - _lane_tiled(run_sc[s, r0:r1, :], tk))
                if mask is not None:
                    w = jnp.where(mask, w, 0.0)
                w_out.append(w.astype(BF16))
            wb = jnp.concatenate([jnp.concatenate(w_out[:half], axis=0),
                                  jnp.concatenate(w_out[half:], axis=0)], axis=1)
            run_sc[s] = run_sc[s] + sums[s]
            acc_sc[s] = acc_sc[s] + _dot(wb, vs[s])

    def full_block(i, _):
        block(qi * n_sub - 1 - i)
        return 0

    for sub in reversed(range(n_sub)):
        block(qi * n_sub + sub, diag_sub=sub)
    lax.fori_loop(0, qi * n_sub, full_block, 0)
    for s, ln in enumerate(lanes):
        o_ref[:, ln] = acc_sc[s]


def _prompt_attention(body, zb, extra, q_blk, k_blk, v_blk, scratch, name):
    b, t, _ = zb.shape
    assert t % TQ == 0 and TQ % TK_DIFF == 0 and TQ % TK_SB == 0
    width = STREAMS * LANES
    assert W_GROUP % width == 0
    n_grp = W_GROUP // width
    extra_specs = [pl.BlockSpec(e.shape, lambda bi, g, qi, nd=e.ndim: (0,) * nd) for e in extra]
    return pl.pallas_call(
        body,
        grid=(b, n_grp, t // TQ),
        in_specs=[pl.BlockSpec((None, TQ, width), lambda bi, g, qi: (bi, qi, q_blk * n_grp + g)),
                  pl.BlockSpec((None, t, width), lambda bi, g, qi: (bi, 0, k_blk * n_grp + g)),
                  pl.BlockSpec((None, t, width), lambda bi, g, qi: (bi, 0, v_blk * n_grp + g))]
        + extra_specs,
        out_specs=pl.BlockSpec((None, TQ, width), lambda bi, g, qi: (bi, qi, g)),
        out_shape=jax.ShapeDtypeStruct((b, t, W_GROUP), F32),
        scratch_shapes=scratch,
        compiler_params=pltpu.CompilerParams(
            dimension_semantics=("parallel", "parallel", "parallel"),
            vmem_limit_bytes=VMEM_LIMIT),
        name=name,
    )(zb, zb, zb, *extra)


def _sample_kernel(pt_ref, qa_ref, qb_ref, kna_ref, vna_ref, knb_ref, vnb_ref, *rest,
                   past, page, lam_init):
    del pt_ref
    n_cached = 4 * PAGES_PER_STEP
    pools = [rest[i * PAGES_PER_STEP:(i + 1) * PAGES_PER_STEP] for i in range(4)]
    lam_ref, g_ref, oa_ref, ob_ref, m_sc, l_sc, acca_sc, run_sc, accb_sc = rest[n_cached:]
    j = pl.program_id(1)
    n_pages = past // page
    rows_h = 2 * Q_PAD
    r = lax.broadcasted_iota(jnp.int32, (ROWS_S, page), 0)
    c = lax.broadcasted_iota(jnp.int32, (ROWS_S, page), 1)
    col = c.astype(F32)
    q_of_row = r & (Q_PAD - 1)
    slope = _pow2_neg(ALIBI_LOG2_STEP * ((r // rows_h) + 1))
    tri = _strict_lower_ones(page)

    def attend(blocks, masked):
        qa = qa_ref[...]
        qb = qb_ref[...]
        scores = [_dot(qa, blk[0][...].astype(BF16)) for blk in blocks]
        logits = [_dot(qb, blk[2][...].astype(BF16)) for blk in blocks]

        for i, blk in enumerate(blocks):
            scores[i] = scores[i] + slope * (col + blk[4])
            if masked:
                scores[i] = jnp.where(c <= q_of_row, scores[i], NEG_INF)
        m = m_sc[...]
        m_new = jnp.maximum(m, jnp.max(functools.reduce(jnp.maximum, scores), axis=-1, keepdims=True))
        alpha = jnp.exp(m - m_new)
        probs = [jnp.exp(s - m_new) for s in scores]
        m_sc[...] = m_new
        l_sc[...] = alpha * l_sc[...] + jnp.sum(functools.reduce(jnp.add, probs), axis=-1,
                                                keepdims=True)
        probs = [p.astype(BF16) for p in probs]

        mask = (c < q_of_row) if masked else None
        log_beta, drops, sums = [], [], []
        for z in logits:
            drop = _softplus(z)
            log_beta.append(z - drop)
            if masked:
                drop = jnp.where(mask, drop, 0.0)
            drops.append(drop.astype(BF16))
            sums.append(jnp.sum(drop, axis=-1, keepdims=True))
        after = [_dot(d, tri) for d in drops]

        for h in range(H_A):
            rows = slice(h * rows_h, (h + 1) * rows_h)
            upd = None
            for blk, p in zip(blocks, probs):
                v_h = blk[1][pl.ds(h, page, stride=H_A), :].astype(BF16)
                d = _dot(p[rows], v_h)
                upd = d if upd is None else upd + d
            acca_sc[rows, :] = alpha[rows] * acca_sc[rows, :] + upd

        run = run_sc[...]
        weights = []
        for lb, aft, rs in zip(log_beta, after, sums):
            w = jnp.exp(lb - aft - run)
            if masked:
                w = jnp.where(mask, w, 0.0)
            weights.append(w.astype(BF16))
            run = run + rs
        run_sc[...] = run
        upd = None
        for blk, w in zip(blocks, weights):
            d = _nt_dot(w, blk[3][...].astype(BF16))
            upd = d if upd is None else upd + d
        accb_sc[...] = accb_sc[...] + upd

    @pl.when(j == 0)
    def _():
        m_sc[...] = jnp.full(m_sc.shape, NEG_INF, F32)
        l_sc[...] = jnp.zeros(l_sc.shape, F32)
        acca_sc[...] = jnp.zeros(acca_sc.shape, F32)
        run_sc[...] = jnp.zeros(run_sc.shape, F32)
        accb_sc[...] = jnp.zeros(accb_sc.shape, F32)
        attend([(kna_ref, vna_ref, knb_ref, vnb_ref, 0.0)], True)

    @pl.when(j > 0)
    def _():
        newest = n_pages - 1 - (j - 1) * PAGES_PER_STEP
        attend([(pools[0][i], pools[1][i], pools[2][i], pools[3][i],
                 ((newest - i) * page - past).astype(F32)) for i in range(PAGES_PER_STEP)], False)

    @pl.when(j == n_pages // PAGES_PER_STEP)
    def _():
        o = acca_sc[...] / l_sc[...]
        lam = _lambda(lam_ref, lam_init)
        for h in range(H_A):
            d = (o[h * rows_h:h * rows_h + Q_PAD]
                 - lam * o[h * rows_h + Q_PAD:(h + 1) * rows_h])
            oa_ref[:, h * DV_A:(h + 1) * DV_A] = _rms(d, g_ref[...]) * (1.0 - lam_init)
        acc = accb_sc[...]
        lane_head = lax.broadcasted_iota(jnp.int32, (Q_PAD, W_GROUP), 1) // DH_B
        out = jnp.zeros((Q_PAD, W_GROUP), F32)
        for h in range(H_B):
            out = out + jnp.where(lane_head == h, acc[h * Q_PAD:(h + 1) * Q_PAD], 0.0)
        ob_ref[...] = out


def _sample_attention(page_table, qbd_a, qbd_b, kn_a, vn_a, kn_b, vn_b,
                      pool_arrays, lam_pack, g_subln, pool_base, lam_init):
    db, n_pages = page_table.shape
    page = pool_arrays[0].shape[2]
    past = n_pages * page
    assert n_pages % PAGES_PER_STEP == 0

    per_b = lambda b, j, pt: (b, 0, 0)

    def paged(i):
        def index(b, j, pt):
            newest = n_pages - 1 - (jnp.maximum(j, 1) - 1) * PAGES_PER_STEP
            return (pool_base + pt[b, newest - i], 0, 0)
        return pl.BlockSpec((None, W_GROUP, page), index)

    new_spec = pl.BlockSpec((None, W_GROUP, page), per_b)
    q_spec = pl.BlockSpec((None, ROWS_S, W_GROUP), per_b)
    out_spec = pl.BlockSpec((None, Q_PAD, W_GROUP), per_b)
    out = jax.ShapeDtypeStruct((db, Q_PAD, W_GROUP), F32)
    cache_specs = [paged(i) for _ in pool_arrays for i in range(PAGES_PER_STEP)]
    cache_args = [a for a in pool_arrays for _ in range(PAGES_PER_STEP)]
    return pl.pallas_call(
        functools.partial(_sample_kernel, past=past, page=page, lam_init=lam_init),
        grid_spec=pltpu.PrefetchScalarGridSpec(
            num_scalar_prefetch=1,
            grid=(db, n_pages // PAGES_PER_STEP + 1),
            in_specs=[q_spec, q_spec, new_spec, new_spec, new_spec, new_spec] + cache_specs
            + [pl.BlockSpec(lam_pack.shape, lambda b, j, pt: (0, 0, 0)),
               pl.BlockSpec(g_subln.shape, lambda b, j, pt: (0, 0))],
            out_specs=[out_spec, out_spec],
            scratch_shapes=[pltpu.VMEM((ROWS_S, 1), F32), pltpu.VMEM((ROWS_S, 1), F32),
                            pltpu.VMEM((ROWS_S, DV_A), F32), pltpu.VMEM((ROWS_S, 1), F32),
                            pltpu.VMEM((ROWS_S, W_GROUP), F32)]),
        out_shape=[out, out],
        compiler_params=pltpu.CompilerParams(dimension_semantics=("parallel", "arbitrary"),
                                             vmem_limit_bytes=VMEM_LIMIT),
        name="sample_attention",
    )(page_table, qbd_a, qbd_b, kn_a, vn_a, kn_b, vn_b, *cache_args, lam_pack, g_subln)


def _sigmoid(x):
    return 1.0 / (1.0 + jnp.exp(-x))


def _merge_kernel(x_ref, oa_ref, ob_ref, gate_ref, p_ref, wo_ref, gp_ref, wg_ref, wp_ref, gf_ref,
                  y_ref, *, final):
    gate = gate_ref[...]
    sg = gate * _sigmoid(gate)
    ua = (oa_ref[...] * sg[:, :W_GROUP]).astype(BF16)
    ub = (ob_ref[...] * sg[:, W_GROUP:]).astype(BF16)
    x1 = x_ref[...] + _dot(ua, wo_ref[:W_GROUP, :]) + _dot(ub, wo_ref[W_GROUP:, :])
    hn = _rms(x1, gp_ref[...]).astype(BF16)
    g = _sigmoid(_dot(hn, wg_ref[...]))
    pe = _dot(p_ref[...].astype(BF16), wp_ref[...])
    x2 = x1 + g * pe
    y_ref[...] = _rms(x2, gf_ref[...]) if final else x2


def _merge(x, o_a, o_b, gate, p, w_out, g_ple, w_gate, w_proj, g_final, tm, final):
    n, d = x.shape
    assert n % tm == 0
    row = lambda i: (i, 0)
    fixed = lambda i: (0, 0)
    full = lambda a: pl.BlockSpec(a.shape, fixed)
    g_ple = g_ple.reshape(1, d)
    g_final = g_final.reshape(1, d)
    return pl.pallas_call(
        functools.partial(_merge_kernel, final=final),
        grid=(n // tm,),
        in_specs=[pl.BlockSpec((tm, d), row), pl.BlockSpec((tm, W_GROUP), row),
                  pl.BlockSpec((tm, W_GROUP), row), pl.BlockSpec((tm, MIX), row),
                  pl.BlockSpec((tm, p.shape[1]), row), full(w_out), full(g_ple), full(w_gate),
                  full(w_proj), full(g_final)],
        out_specs=pl.BlockSpec((tm, d), row),
        out_shape=jax.ShapeDtypeStruct((n, d), F32),
        compiler_params=pltpu.CompilerParams(dimension_semantics=("parallel",),
                                             vmem_limit_bytes=VMEM_LIMIT),
        name="merge",
    )(x, o_a, o_b, gate, p, w_out, g_ple, w_gate, w_proj, g_final)


def _block_diag_queries(q, group_width):
    n_groups = W_GROUP // group_width
    lane_group = jnp.arange(W_GROUP, dtype=jnp.int32) // group_width
    mask = (lane_group[None, :] == jnp.arange(n_groups, dtype=jnp.int32)[:, None]).astype(q.dtype)
    out = q[:, None, :, :] * mask[None, :, None, :]
    return out.reshape(q.shape[0], n_groups * Q_PAD, W_GROUP)


def _tokens_major(kt, lead, tail):
    nb, _, rows = kt.shape
    a = jnp.moveaxis(kt.reshape(nb, *tail, rows), -1, 1)
    return a.reshape(*lead, *tail)


def kernel(x_prompt, x_sample, cache_k_diff, cache_v_diff, cache_k_sb, cache_v_sb, page_table,
           p_prompt, p_sample, g_norm, w_in, lambda_q1, lambda_k1, lambda_q2, lambda_k2,
           g_subln, w_out, g_ple, w_ple_gate, w_ple_proj, g_final):
    b, t, d = x_prompt.shape
    db, ds, _ = x_sample.shape
    depth, n_pool, page = cache_k_diff.shape[:3]
    assert ds <= Q_PAD and page * H_A == W_GROUP and DV_A == page

    xp = x_prompt.reshape(b * t, d)
    xs = x_sample.reshape(db * ds, d)
    n_all = depth * n_pool
    pools = [cache_k_diff.transpose(0, 1, 3, 4, 5, 2).reshape(n_all, W_GROUP, page),
             cache_v_diff.reshape(n_all, page * H_A, DV_A),
             cache_k_sb.transpose(0, 1, 3, 4, 2).reshape(n_all, W_GROUP, page),
             cache_v_sb.transpose(0, 1, 3, 4, 2).reshape(n_all, W_GROUP, page)]

    kd, vd, ksb = (H_A, 2, DH_A), (H_A, DV_A), (H_B, DH_B)
    new_p = [[] for _ in range(4)]
    new_s = [[] for _ in range(4)]
    for l in range(depth):
        lam_init = 0.8 - 0.6 * math.exp(-0.3 * l)
        final = l == depth - 1
        w_in_l = w_in[l].astype(BF16)
        w_out_l = w_out[l].astype(BF16)
        w_gate_l = w_ple_gate[l].astype(BF16)
        w_proj_l = w_ple_proj[l].astype(BF16)
        lam_pack = jnp.stack([lambda_q1[l], lambda_k1[l], lambda_q2[l], lambda_k2[l]])[:, None, :]
        g_sub = g_subln[l].reshape(1, DV_A)

        zb, kta, va, ktb, vtb, gate = _proj(xp, g_norm[l], w_in_l, 512, t)
        zb3 = zb.reshape(b, t, N_GROUPS_QKV * W_GROUP)
        o_a = _prompt_attention(functools.partial(_diff_prompt_kernel, lam_init=lam_init),
                                zb3, [lam_pack, g_sub], 0, 1, 2,
                                [pltpu.VMEM((STREAMS, 2 * TQ, LANES), F32),
                                 pltpu.VMEM((STREAMS, 2 * TQ, 2 * LANES), F32)], "diff_prompt")
        o_b = _prompt_attention(_sb_prompt_kernel, zb3, [], 3, 4, 5,
                                [pltpu.VMEM((STREAMS, 2 * TQ, LANES), F32),
                                 pltpu.VMEM((STREAMS, TQ, LANES), F32)], "sb_prompt")
        xp = _merge(xp, o_a.reshape(b * t, W_GROUP), o_b.reshape(b * t, W_GROUP), gate,
                    p_prompt[l].reshape(b * t, -1), w_out_l, g_ple[l], w_gate_l, w_proj_l,
                    g_final, 512, final)
        new_p[0].append(_tokens_major(kta, (b, t), kd))
        new_p[1].append(va.reshape(b, t, *vd))
        new_p[2].append(_tokens_major(ktb, (b, t), ksb))
        new_p[3].append(_tokens_major(vtb, (b, t), ksb))

        zb, kta, va, ktb, vtb, gate = _proj(xs, g_norm[l], w_in_l, db * ds, db * ds)
        zs = zb.reshape(db, ds, N_GROUPS_QKV, W_GROUP)
        q_pad = lambda i: jnp.pad(zs[:, :, i], ((0, 0), (0, Q_PAD - ds), (0, 0)))
        per_b = lambda kt: jnp.pad(kt.reshape(W_GROUP, db, ds).transpose(1, 0, 2),
                                   ((0, 0), (0, 0), (0, page - ds)))
        vn_a = jnp.pad(va.reshape(db, ds, H_A, DV_A), ((0, 0), (0, page - ds), (0, 0), (0, 0)))
        so_a, so_b = _sample_attention(
            page_table, _block_diag_queries(q_pad(0), DH_A), _block_diag_queries(q_pad(3), DH_B),
            per_b(kta), vn_a.reshape(db, page * H_A, DV_A), per_b(ktb), per_b(vtb),
            pools, lam_pack, g_sub, l * n_pool, lam_init)
        xs = _merge(xs, so_a[:, :ds].reshape(db * ds, W_GROUP), so_b[:, :ds].reshape(db * ds, W_GROUP),
                    gate, p_sample[l].reshape(db * ds, -1), w_out_l, g_ple[l], w_gate_l, w_proj_l,
                    g_final, db * ds, final)
        new_s[0].append(_tokens_major(kta, (db, ds), kd))
        new_s[1].append(va.reshape(db, ds, *vd))
        new_s[2].append(_tokens_major(ktb, (db, ds), ksb))
        new_s[3].append(_tokens_major(vtb, (db, ds), ksb))

    stacked = lambda parts: jnp.stack(parts) if depth > 1 else parts[0][None]
    return (xp.reshape(b, t, d), xs.reshape(db, ds, d),
            *[stacked(parts) for parts in new_p], *[stacked(parts) for parts in new_s])
```

```python
import functools
import math

import jax
import jax.numpy as jnp
from jax import lax
from jax.experimental import pallas as pl
from jax.experimental.pallas import tpu as pltpu

F32 = jnp.float32
BF16 = jnp.bfloat16

H_A = 4
DH_A = 64
DV_A = 2 * DH_A
H_B = 8
DH_B = 64
EPS = 1e-6
NEG_INF = -1e30
LOG2_E = math.log2(math.e)
W_GROUP = H_A * DV_A
N_GROUPS_QKV = 6
TOKEN_MINOR_GROUPS = (1, 4, 5)
MIX = 2 * W_GROUP

LANES = 128
SUBLANES = 8
HALF = LANES // 2
VMEM_LIMIT = 56 * 1024 * 1024

ALIBI_LOG2_STEP = 8 // H_A
assert 8 % H_A == 0

TQ = 512
TK_SB = 256
ROW_CHUNK = 128
STREAMS = 4
DIFF_STREAMS = 2
Q_PAD = SUBLANES
ROWS_S = 2 * H_A * Q_PAD
assert ROWS_S == H_B * Q_PAD
PAGES_PER_STEP = 8


def _pow2_neg(e):
    return lax.bitcast_convert_type((127 - e) << 23, F32)


def _rms(x, g):
    return x * lax.rsqrt(jnp.mean(x * x, axis=-1, keepdims=True) + EPS) * g


def _nt_dot(a, b):
    return lax.dot_general(a, b, (((1,), (1,)), ((), ())), preferred_element_type=F32)


def _dot(a, b):
    return jnp.dot(a, b, preferred_element_type=F32)


def _lambda(lam_ref, lam_init):
    t1 = jnp.sum(lam_ref[0] * lam_ref[1], axis=-1, keepdims=True)
    t2 = jnp.sum(lam_ref[2] * lam_ref[3], axis=-1, keepdims=True)
    return jnp.exp(t1) - jnp.exp(t2) + lam_init


def _softplus(z):
    return jnp.maximum(z, 0.0) + jnp.log(1.0 + jnp.exp2(jnp.abs(z) * (-LOG2_E)))


def _strict_lower_ones(n):
    r = lax.broadcasted_iota(jnp.int32, (n, n), 0)
    c = lax.broadcasted_iota(jnp.int32, (n, n), 1)
    return jnp.where(r > c, 1.0, 0.0).astype(BF16)


def _proj_kernel(x_ref, g_ref, w_ref, zb_ref, kta_ref, va_ref, ktb_ref, vtb_ref, gate_ref):
    h = _rms(x_ref[...], g_ref[...]).astype(BF16)
    tm = h.shape[0]

    def cols(i):
        return _dot(h, w_ref[:, i * W_GROUP:(i + 1) * W_GROUP])

    token_minor = dict(zip(TOKEN_MINOR_GROUPS, (kta_ref, ktb_ref, vtb_ref)))
    for i in range(N_GROUPS_QKV):
        z = cols(i)
        if i == 2:
            for hh in range(H_A):
                va_ref[pl.ds(hh, tm, stride=H_A), :] = z[:, hh * DV_A:(hh + 1) * DV_A]
        if i in token_minor:
            token_minor[i][...] = z.T
        if i in (0, 3):
            z = z * (DH_A ** -0.5)
        zb_ref[:, i * W_GROUP:(i + 1) * W_GROUP] = z.astype(BF16)
    for i in range(2):
        gate_ref[:, i * W_GROUP:(i + 1) * W_GROUP] = cols(N_GROUPS_QKV + i)


def _proj(x, g, w_bf16, tm, rows_per_batch):
    n, d = x.shape
    d_in = w_bf16.shape[1]
    assert n % rows_per_batch == 0 and rows_per_batch % tm == 0
    assert d_in == (N_GROUPS_QKV + 2) * W_GROUP
    tiles_per_batch = rows_per_batch // tm
    row = lambda i: (i, 0)
    fixed = lambda i: (0, 0)
    tok_minor = lambda i: (i // tiles_per_batch, 0, i % tiles_per_batch)
    kt = jax.ShapeDtypeStruct((n // rows_per_batch, W_GROUP, rows_per_batch), F32)
    kt_spec = pl.BlockSpec((None, W_GROUP, tm), tok_minor)
    return pl.pallas_call(
        _proj_kernel,
        grid=(n // tm,),
        in_specs=[pl.BlockSpec((tm, d), row), pl.BlockSpec((1, d), fixed),
                  pl.BlockSpec((d, d_in), fixed)],
        out_specs=[pl.BlockSpec((tm, N_GROUPS_QKV * W_GROUP), row), kt_spec,
                   pl.BlockSpec((tm * H_A, DV_A), row), kt_spec, kt_spec,
                   pl.BlockSpec((tm, MIX), row)],
        out_shape=[jax.ShapeDtypeStruct((n, N_GROUPS_QKV * W_GROUP), BF16), kt,
                   jax.ShapeDtypeStruct((n * H_A, DV_A), F32), kt, kt,
                   jax.ShapeDtypeStruct((n, MIX), F32)],
        compiler_params=pltpu.CompilerParams(dimension_semantics=("parallel",),
                                             vmem_limit_bytes=VMEM_LIMIT),
        name="proj",
    )(x, g.reshape(1, d), w_bf16)


def _stack_halves(x):
    lane = lax.broadcasted_iota(jnp.int32, x.shape, 1)
    zero = jnp.zeros_like(x)
    return jnp.concatenate([jnp.where(lane < HALF, x, zero), jnp.where(lane >= HALF, x, zero)], axis=0)


def _row_chunks():
    return [(r0, r0 + ROW_CHUNK) for r0 in range(0, 2 * TQ, ROW_CHUNK)]


def _causal_iotas(r0, tk):
    r = (lax.broadcasted_iota(jnp.int32, (ROW_CHUNK, tk), 0) + r0) & (TQ - 1)
    c = lax.broadcasted_iota(jnp.int32, (ROW_CHUNK, tk), 1)
    return r, c


def _lane_replicated(x):
    return jnp.broadcast_to(x, (x.shape[0], LANES))


def _lane_tiled(x, width):
    return jnp.concatenate([x] * (width // LANES), axis=1)


def _stream_lanes():
    return [slice(s * LANES, (s + 1) * LANES) for s in range(STREAMS)]


def _run_interleaved(*stage_generators):
    live = list(stage_generators)
    while live:
        for gen in list(live):
            try:
                next(gen)
            except StopIteration:
                live.remove(gen)


def _diff_block_stages(q_ref, k_ref, v_ref, m_sc, acc_sc, head0, kj, key_offset, diag):
    tk = TQ
    lanes = [slice(s * LANES, (s + 1) * LANES) for s in range(DIFF_STREAMS)]
    chunks = _row_chunks()
    qs = [_stack_halves(q_ref[:, ln]) for ln in lanes]
    slope = [_pow2_neg(jnp.full((1, tk), ALIBI_LOG2_STEP, jnp.int32) * (head0 + s + 1))
             for s in range(DIFF_STREAMS)]
    col = lax.broadcasted_iota(jnp.int32, (1, tk), 1).astype(F32)
    ones = jnp.ones((tk, LANES), BF16)
    start = pl.multiple_of(kj * tk, tk)
    s_all = [_nt_dot(qs[s], k_ref[pl.ds(start, tk), ln]) for s, ln in enumerate(lanes)]
    vs = [jnp.concatenate([v_ref[pl.ds(start, tk), ln], ones], axis=1) for ln in lanes]
    yield
    bias = [sl * (col + key_offset) for sl in slope]

    def scores(s, r0, r1):
        x = s_all[s][r0:r1] + bias[s]
        if diag:
            r, c = _causal_iotas(r0, tk)
            x = jnp.where(c <= r, x, NEG_INF)
        return x

    m_new = [jnp.concatenate(
        [jnp.maximum(m_sc[s, r0:r1, :],
                     _lane_replicated(jnp.max(scores(s, r0, r1), axis=-1, keepdims=True)))
         for r0, r1 in chunks], axis=0) for s in range(DIFF_STREAMS)]
    p = [jnp.concatenate(
        [jnp.exp(scores(s, r0, r1) - _lane_tiled(m_new[s][r0:r1], tk)).astype(BF16)
         for r0, r1 in chunks], axis=0) for s in range(DIFF_STREAMS)]
    yield
    for s in range(DIFF_STREAMS):
        alpha = jnp.exp(m_sc[s] - m_new[s])
        acc_sc[s] = _lane_tiled(alpha, 2 * LANES) * acc_sc[s] + _dot(p[s], vs[s])
        m_sc[s] = m_new[s]


def _diff_finalize(acc_sc, lam_ref, g_ref, o_ref, lam_init):
    lam = _lambda(lam_ref, lam_init)
    for s in range(DIFF_STREAMS):
        acc = acc_sc[s]
        o = acc[:, :LANES] / acc[:, LANES:]
        d = o[:TQ] - lam * o[TQ:]
        o_ref[:, s * LANES:(s + 1) * LANES] = _rms(d, g_ref[...]) * (1.0 - lam_init)


def _sb_prompt_kernel(q_ref, k_ref, v_ref, o_ref, run_sc, acc_sc):
    tk = TK_SB
    n_sub = TQ // tk
    qi = pl.program_id(2)
    lanes = _stream_lanes()
    chunks = _row_chunks()
    half = len(chunks) // 2
    qs = [_stack_halves(q_ref[:, ln]) for ln in lanes]
    tri = _strict_lower_ones(tk)
    run_sc[...] = jnp.zeros(run_sc.shape, F32)
    acc_sc[...] = jnp.zeros(acc_sc.shape, F32)

    def block(kj, diag_sub=None):
        start = pl.multiple_of(kj * tk, tk)
        z_all = [_nt_dot(qs[s], k_ref[pl.ds(start, tk), ln]) for s, ln in enumerate(lanes)]
        vs = [_stack_halves(v_ref[pl.ds(start, tk), ln]) for ln in lanes]
        masks = [None] * len(chunks)
        if diag_sub is not None:
            for i, (r0, _) in enumerate(chunks):
                r, c = _causal_iotas(r0, tk)
                masks[i] = c + diag_sub * tk < r
        log_beta, drops, sums = [], [], []
        for s in range(STREAMS):
            lb_s, drop_s, sum_s = [], [], []
            for (r0, r1), mask in zip(chunks, masks):
                z = z_all[s][r0:r1]
                drop = _softplus(z)
                lb_s.append(z - drop)
                if mask is not None:
                    drop = jnp.where(mask, drop, 0.0)
                drop_s.append(drop.astype(BF16))
                sum_s.append(_lane_replicated(jnp.sum(drop, axis=-1, keepdims=True)))
            log_beta.append(lb_s)
            drops.append(jnp.concatenate(drop_s, axis=0))
            sums.append(jnp.concatenate(sum_s, axis=0))
        after = [_dot(drops[s], tri) for s in range(STREAMS)]
        for s in range(STREAMS):
            w_out = []
            for (r0, r1), mask, lb in zip(chunks, masks, log_beta[s]):
                w = jnp.exp(lb - after[s][r0:r1] - _lane_tiled(run_sc[s, r0:r1, :], tk))
                if mask is not None:
                    w = jnp.where(mask, w, 0.0)
                w_out.append(w.astype(BF16))
            wb = jnp.concatenate([jnp.concatenate(w_out[:half], axis=0),
                                  jnp.concatenate(w_out[half:], axis=0)], axis=1)
            run_sc[s] = run_sc[s] + sums[s]
            acc_sc[s] = acc_sc[s] + _dot(wb, vs[s])

    def full_block(i, _):
        block(qi * n_sub - 1 - i)
        return 0

    for sub in reversed(range(n_sub)):
        block(qi * n_sub + sub, diag_sub=sub)
    lax.fori_loop(0, qi * n_sub, full_block, 0)
    for s, ln in enumerate(lanes):
        o_ref[:, ln] = acc_sc[s]


def _prompt_attention(body, zb, extra, q_blk, k_blk, v_blk, scratch, name):
    b, t, _ = zb.shape
    assert t % TQ == 0 and TQ % TK_SB == 0
    width = STREAMS * LANES
    assert W_GROUP % width == 0
    n_grp = W_GROUP // width
    extra_specs = [pl.BlockSpec(e.shape, lambda bi, g, qi, nd=e.ndim: (0,) * nd) for e in extra]
    return pl.pallas_call(
        body,
        grid=(b, n_grp, t // TQ),
        in_specs=[pl.BlockSpec((None, TQ, width), lambda bi, g, qi: (bi, qi, q_blk * n_grp + g)),
                  pl.BlockSpec((None, t, width), lambda bi, g, qi: (bi, 0, k_blk * n_grp + g)),
                  pl.BlockSpec((None, t, width), lambda bi, g, qi: (bi, 0, v_blk * n_grp + g))]
        + extra_specs,
        out_specs=pl.BlockSpec((None, TQ, width), lambda bi, g, qi: (bi, qi, g)),
        out_shape=jax.ShapeDtypeStruct((b, t, W_GROUP), F32),
        scratch_shapes=scratch,
        compiler_params=pltpu.CompilerParams(
            dimension_semantics=("parallel", "parallel", "parallel"),
            vmem_limit_bytes=VMEM_LIMIT),
        name=name,
    )(zb, zb, zb, *extra)


def _diff_sample_kernel(pt_ref, qtab_ref, ktab_ref, q_ref, k_ref, v_ref, lam_ref, g_ref,
                        qa_ref, qb_ref, kna_ref, vna_ref, knb_ref, vnb_ref, *rest,
                        past, page, lam_init, blocks_per_group, n_groups, sample_steps):
    del pt_ref
    n_cached = 4 * PAGES_PER_STEP
    pools = [rest[i * PAGES_PER_STEP:(i + 1) * PAGES_PER_STEP] for i in range(4)]
    (o_ref, oa_ref, ob_ref, dm_sc, dacc_sc,
     m_sc, l_sc, acca_sc, run_sc, accb_sc) = rest[n_cached:]
    t = pl.program_id(0)
    n_pages = past // page
    steps_per_row = n_pages // PAGES_PER_STEP + 1
    slot = t % blocks_per_group
    qi = qtab_ref[slot]
    kb = ktab_ref[slot]
    head0 = ((t // blocks_per_group) % n_groups) * DIFF_STREAMS
    j = t % steps_per_row
    sample_live = t < sample_steps
    rows_h = 2 * Q_PAD
    r = lax.broadcasted_iota(jnp.int32, (ROWS_S, page), 0)
    c = lax.broadcasted_iota(jnp.int32, (ROWS_S, page), 1)
    col = c.astype(F32)
    q_of_row = r & (Q_PAD - 1)
    slope = _pow2_neg(ALIBI_LOG2_STEP * ((r // rows_h) + 1))
    tri = _strict_lower_ones(page)

    def sample_stages(blocks, masked):
        qa = qa_ref[...]
        qb = qb_ref[...]
        scores = [_dot(qa, blk[0][...].astype(BF16)) for blk in blocks]
        logits = [_dot(qb, blk[2][...].astype(BF16)) for blk in blocks]
        yield

        for i, blk in enumerate(blocks):
            scores[i] = scores[i] + slope * (col + blk[4])
            if masked:
                scores[i] = jnp.where(c <= q_of_row, scores[i], NEG_INF)
        m = m_sc[...]
        m_new = jnp.maximum(m, jnp.max(functools.reduce(jnp.maximum, scores), axis=-1, keepdims=True))
        alpha = jnp.exp(m - m_new)
        probs = [jnp.exp(s - m_new) for s in scores]
        m_sc[...] = m_new
        l_sc[...] = alpha * l_sc[...] + jnp.sum(functools.reduce(jnp.add, probs), axis=-1,
                                                keepdims=True)
        probs = [p.astype(BF16) for p in probs]

        mask = (c < q_of_row) if masked else None
        log_beta, drops, sums = [], [], []
        for z in logits:
            drop = _softplus(z)
            log_beta.append(z - drop)
            if masked:
                drop = jnp.where(mask, drop, 0.0)
            drops.append(drop.astype(BF16))
            sums.append(jnp.sum(drop, axis=-1, keepdims=True))
        after = [_dot(d, tri) for d in drops]
        yield

        for h in range(H_A):
            rows = slice(h * rows_h, (h + 1) * rows_h)
            upd = None
            for blk, p in zip(blocks, probs):
                v_h = blk[1][pl.ds(h, page, stride=H_A), :].astype(BF16)
                d = _dot(p[rows], v_h)
                upd = d if upd is None else upd + d
            acca_sc[rows, :] = alpha[rows] * acca_sc[rows, :] + upd

        run = run_sc[...]
        weights = []
        for lb, aft, rs in zip(log_beta, after, sums):
            w = jnp.exp(lb - aft - run)
            if masked:
                w = jnp.where(mask, w, 0.0)
            weights.append(w.astype(BF16))
            run = run + rs
        run_sc[...] = run
        upd = None
        for blk, w in zip(blocks, weights):
            d = _nt_dot(w, blk[3][...].astype(BF16))
            upd = d if upd is None else upd + d
        accb_sc[...] = accb_sc[...] + upd

    def page_stages():
        newest = n_pages - 1 - (j - 1) * PAGES_PER_STEP
        return sample_stages(
            [(pools[0][i], pools[1][i], pools[2][i], pools[3][i],
              ((newest - i) * page - past).astype(F32)) for i in range(PAGES_PER_STEP)], False)

    def prompt_stages(diag):
        return _diff_block_stages(q_ref, k_ref, v_ref, dm_sc, dacc_sc, head0, kb,
                                  ((kb - qi) * TQ).astype(F32), diag)

    is_diag = kb == qi
    has_pages = jnp.logical_and(sample_live, j > 0)

    @pl.when(kb == 0)
    def _():
        dm_sc[...] = jnp.full(dm_sc.shape, NEG_INF, F32)
        dacc_sc[...] = jnp.zeros(dacc_sc.shape, F32)

    @pl.when(jnp.logical_and(sample_live, j == 0))
    def _():
        m_sc[...] = jnp.full(m_sc.shape, NEG_INF, F32)
        l_sc[...] = jnp.zeros(l_sc.shape, F32)
        acca_sc[...] = jnp.zeros(acca_sc.shape, F32)
        run_sc[...] = jnp.zeros(run_sc.shape, F32)
        accb_sc[...] = jnp.zeros(accb_sc.shape, F32)
        _run_interleaved(sample_stages([(kna_ref, vna_ref, knb_ref, vnb_ref, 0.0)], True))

    @pl.when(jnp.logical_and(jnp.logical_not(is_diag), has_pages))
    def _():
        _run_interleaved(prompt_stages(False), page_stages())

    @pl.when(jnp.logical_and(jnp.logical_not(is_diag), jnp.logical_not(has_pages)))
    def _():
        _run_interleaved(prompt_stages(False))

    @pl.when(is_diag)
    def _():
        _run_interleaved(prompt_stages(True))
        _diff_finalize(dacc_sc, lam_ref, g_ref, o_ref, lam_init)

    @pl.when(jnp.logical_and(is_diag, has_pages))
    def _():
        _run_interleaved(page_stages())

    @pl.when(jnp.logical_and(sample_live, j == steps_per_row - 1))
    def _():
        o = acca_sc[...] / l_sc[...]
        lam = _lambda(lam_ref, lam_init)
        for h in range(H_A):
            d = (o[h * rows_h:h * rows_h + Q_PAD]
                 - lam * o[h * rows_h + Q_PAD:(h + 1) * rows_h])
            oa_ref[:, h * DV_A:(h + 1) * DV_A] = _rms(d, g_ref[...]) * (1.0 - lam_init)
        acc = accb_sc[...]
        lane_head = lax.broadcasted_iota(jnp.int32, (Q_PAD, W_GROUP), 1) // DH_B
        out = jnp.zeros((Q_PAD, W_GROUP), F32)
        for h in range(H_B):
            out = out + jnp.where(lane_head == h, acc[h * Q_PAD:(h + 1) * Q_PAD], 0.0)
        ob_ref[...] = out


def _diff_and_sample_attention(zb, page_table, qbd_a, qbd_b, kn_a, vn_a, kn_b, vn_b,
                               pool_arrays, lam_pack, g_subln, pool_base, lam_init):
    b, t, _ = zb.shape
    db, n_pages = page_table.shape
    page = pool_arrays[0].shape[2]
    past = n_pages * page
    assert n_pages % PAGES_PER_STEP == 0 and t % TQ == 0
    width = DIFF_STREAMS * LANES
    n_groups = W_GROUP // width
    nq = t // TQ
    pairs = [(qi, kb) for qi in range(nq) for kb in range(qi + 1)]
    qtab = jnp.array([p[0] for p in pairs], jnp.int32)
    ktab = jnp.array([p[1] for p in pairs], jnp.int32)
    bpg = len(pairs)
    n_steps = b * n_groups * bpg
    steps_per_row = n_pages // PAGES_PER_STEP + 1
    sample_steps = db * steps_per_row
    assert sample_steps <= n_steps, "the prompt sweep must have a step for every page group"

    def prompt_map(col_group):
        def index(s, pt, qt, kt):
            bg = s // bpg
            return (bg // n_groups, 0, col_group * n_groups + bg % n_groups)
        return index

    def q_index(s, pt, qt, kt):
        bg = s // bpg
        return (bg // n_groups, qt[s % bpg], bg % n_groups)

    def row_index(s, pt, qt, kt):
        return (jnp.minimum(s, sample_steps - 1) // steps_per_row, 0, 0)

    def paged(i):
        def index(s, pt, qt, kt):
            s = jnp.minimum(s, sample_steps - 1)
            j = s % steps_per_row
            newest = n_pages - 1 - (jnp.maximum(j, 1) - 1) * PAGES_PER_STEP
            return (pool_base + pt[s // steps_per_row, newest - i], 0, 0)
        return pl.BlockSpec((None, W_GROUP, page), index)

    fixed = lambda nd: (lambda s, pt, qt, kt: (0,) * nd)
    new_spec = pl.BlockSpec((None, W_GROUP, page), row_index)
    q_spec = pl.BlockSpec((None, ROWS_S, W_GROUP), row_index)
    out_spec = pl.BlockSpec((None, Q_PAD, W_GROUP), row_index)
    out = jax.ShapeDtypeStruct((db, Q_PAD, W_GROUP), F32)
    cache_specs = [paged(i) for _ in pool_arrays for i in range(PAGES_PER_STEP)]
    cache_args = [a for a in pool_arrays for _ in range(PAGES_PER_STEP)]
    return pl.pallas_call(
        functools.partial(_diff_sample_kernel, past=past, page=page, lam_init=lam_init,
                          blocks_per_group=bpg, n_groups=n_groups, sample_steps=sample_steps),
        grid_spec=pltpu.PrefetchScalarGridSpec(
            num_scalar_prefetch=3,
            grid=(n_steps,),
            in_specs=[pl.BlockSpec((None, TQ, width), q_index),
                      pl.BlockSpec((None, t, width), prompt_map(1)),
                      pl.BlockSpec((None, t, width), prompt_map(2)),
                      pl.BlockSpec(lam_pack.shape, fixed(3)), pl.BlockSpec(g_subln.shape, fixed(2)),
                      q_spec, q_spec, new_spec, new_spec, new_spec, new_spec] + cache_specs,
            out_specs=[pl.BlockSpec((None, TQ, width), q_index), out_spec, out_spec],
            scratch_shapes=[pltpu.VMEM((DIFF_STREAMS, 2 * TQ, LANES), F32),
                            pltpu.VMEM((DIFF_STREAMS, 2 * TQ, 2 * LANES), F32),
                            pltpu.VMEM((ROWS_S, 1), F32), pltpu.VMEM((ROWS_S, 1), F32),
                            pltpu.VMEM((ROWS_S, DV_A), F32), pltpu.VMEM((ROWS_S, 1), F32),
                            pltpu.VMEM((ROWS_S, W_GROUP), F32)]),
        out_shape=[jax.ShapeDtypeStruct((b, t, W_GROUP), F32), out, out],
        compiler_params=pltpu.CompilerParams(dimension_semantics=("arbitrary",),
                                             vmem_limit_bytes=VMEM_LIMIT),
        name="diff_sample_attention",
    )(page_table, qtab, ktab, zb, zb, zb, lam_pack, g_subln,
      qbd_a, qbd_b, kn_a, vn_a, kn_b, vn_b, *cache_args)


def _sigmoid(x):
    return 1.0 / (1.0 + jnp.exp(-x))


def _merge_kernel(x_ref, oa_ref, ob_ref, gate_ref, p_ref, wo_ref, gp_ref, wg_ref, wp_ref, gf_ref,
                  y_ref, *, final):
    gate = gate_ref[...]
    sg = gate * _sigmoid(gate)
    ua = (oa_ref[...] * sg[:, :W_GROUP]).astype(BF16)
    ub = (ob_ref[...] * sg[:, W_GROUP:]).astype(BF16)
    x1 = x_ref[...] + _dot(ua, wo_ref[:W_GROUP, :]) + _dot(ub, wo_ref[W_GROUP:, :])
    hn = _rms(x1, gp_ref[...]).astype(BF16)
    g = _sigmoid(_dot(hn, wg_ref[...]))
    pe = _dot(p_ref[...].astype(BF16), wp_ref[...])
    x2 = x1 + g * pe
    y_ref[...] = _rms(x2, gf_ref[...]) if final else x2


def _merge(x, o_a, o_b, gate, p, w_out, g_ple, w_gate, w_proj, g_final, tm, final):
    n, d = x.shape
    assert n % tm == 0
    row = lambda i: (i, 0)
    fixed = lambda i: (0, 0)
    full = lambda a: pl.BlockSpec(a.shape, fixed)
    g_ple = g_ple.reshape(1, d)
    g_final = g_final.reshape(1, d)
    return pl.pallas_call(
        functools.partial(_merge_kernel, final=final),
        grid=(n // tm,),
        in_specs=[pl.BlockSpec((tm, d), row), pl.BlockSpec((tm, W_GROUP), row),
                  pl.BlockSpec((tm, W_GROUP), row), pl.BlockSpec((tm, MIX), row),
                  pl.BlockSpec((tm, p.shape[1]), row), full(w_out), full(g_ple), full(w_gate),
                  full(w_proj), full(g_final)],
        out_specs=pl.BlockSpec((tm, d), row),
        out_shape=jax.ShapeDtypeStruct((n, d), F32),
        compiler_params=pltpu.CompilerParams(dimension_semantics=("parallel",),
                                             vmem_limit_bytes=VMEM_LIMIT),
        name="merge",
    )(x, o_a, o_b, gate, p, w_out, g_ple, w_gate, w_proj, g_final)


def _block_diag_queries(q, group_width):
    n_groups = W_GROUP // group_width
    lane_group = jnp.arange(W_GROUP, dtype=jnp.int32) // group_width
    mask = (lane_group[None, :] == jnp.arange(n_groups, dtype=jnp.int32)[:, None]).astype(q.dtype)
    out = q[:, None, :, :] * mask[None, :, None, :]
    return out.reshape(q.shape[0], n_groups * Q_PAD, W_GROUP)


def _tokens_major(kt, lead, tail):
    nb, _, rows = kt.shape
    a = jnp.moveaxis(kt.reshape(nb, *tail, rows), -1, 1)
    return a.reshape(*lead, *tail)


def kernel(x_prompt, x_sample, cache_k_diff, cache_v_diff, cache_k_sb, cache_v_sb, page_table,
           p_prompt, p_sample, g_norm, w_in, lambda_q1, lambda_k1, lambda_q2, lambda_k2,
           g_subln, w_out, g_ple, w_ple_gate, w_ple_proj, g_final):
    b, t, d = x_prompt.shape
    db, ds, _ = x_sample.shape
    depth, n_pool, page = cache_k_diff.shape[:3]
    assert ds <= Q_PAD and page * H_A == W_GROUP and DV_A == page

    xp = x_prompt.reshape(b * t, d)
    xs = x_sample.reshape(db * ds, d)
    n_all = depth * n_pool
    pools = [cache_k_diff.transpose(0, 1, 3, 4, 5, 2).reshape(n_all, W_GROUP, page),
             cache_v_diff.reshape(n_all, page * H_A, DV_A),
             cache_k_sb.transpose(0, 1, 3, 4, 2).reshape(n_all, W_GROUP, page),
             cache_v_sb.transpose(0, 1, 3, 4, 2).reshape(n_all, W_GROUP, page)]

    kd, vd, ksb = (H_A, 2, DH_A), (H_A, DV_A), (H_B, DH_B)
    new_p = [[] for _ in range(4)]
    new_s = [[] for _ in range(4)]
    for l in range(depth):
        lam_init = 0.8 - 0.6 * math.exp(-0.3 * l)
        final = l == depth - 1
        w_in_l = w_in[l].astype(BF16)
        w_out_l = w_out[l].astype(BF16)
        w_gate_l = w_ple_gate[l].astype(BF16)
        w_proj_l = w_ple_proj[l].astype(BF16)
        lam_pack = jnp.stack([lambda_q1[l], lambda_k1[l], lambda_q2[l], lambda_k2[l]])[:, None, :]
        g_sub = g_subln[l].reshape(1, DV_A)

        zb, kta, va, ktb, vtb, gate = _proj(xp, g_norm[l], w_in_l, 512, t)
        zb3 = zb.reshape(b, t, N_GROUPS_QKV * W_GROUP)
        new_p[0].append(_tokens_major(kta, (b, t), kd))
        new_p[1].append(va.reshape(b, t, *vd))
        new_p[2].append(_tokens_major(ktb, (b, t), ksb))
        new_p[3].append(_tokens_major(vtb, (b, t), ksb))
        gate_p = gate
        zb, kta, va, ktb, vtb, gate = _proj(xs, g_norm[l], w_in_l, db * ds, db * ds)
        zs = zb.reshape(db, ds, N_GROUPS_QKV, W_GROUP)
        q_pad = lambda i: jnp.pad(zs[:, :, i], ((0, 0), (0, Q_PAD - ds), (0, 0)))
        per_b = lambda kt: jnp.pad(kt.reshape(W_GROUP, db, ds).transpose(1, 0, 2),
                                   ((0, 0), (0, 0), (0, page - ds)))
        vn_a = jnp.pad(va.reshape(db, ds, H_A, DV_A), ((0, 0), (0, page - ds), (0, 0), (0, 0)))

        o_a, so_a, so_b = _diff_and_sample_attention(
            zb3, page_table, _block_diag_queries(q_pad(0), DH_A),
            _block_diag_queries(q_pad(3), DH_B),
            per_b(kta), vn_a.reshape(db, page * H_A, DV_A), per_b(ktb), per_b(vtb),
            pools, lam_pack, g_sub, l * n_pool, lam_init)
        o_b = _prompt_attention(_sb_prompt_kernel, zb3, [], 3, 4, 5,
                                [pltpu.VMEM((STREAMS, 2 * TQ, LANES), F32),
                                 pltpu.VMEM((STREAMS, TQ, LANES), F32)], "sb_prompt")

        xp = _merge(xp, o_a.reshape(b * t, W_GROUP), o_b.reshape(b * t, W_GROUP), gate_p,
                    p_prompt[l].reshape(b * t, -1), w_out_l, g_ple[l], w_gate_l, w_proj_l,
                    g_final, 512, final)
        xs = _merge(xs, so_a[:, :ds].reshape(db * ds, W_GROUP), so_b[:, :ds].reshape(db * ds, W_GROUP),
                    gate, p_sample[l].reshape(db * ds, -1), w_out_l, g_ple[l], w_gate_l, w_proj_l,
                    g_final, db * ds, final)
        new_s[0].append(_tokens_major(kta, (db, ds), kd))
        new_s[1].append(va.reshape(db, ds, *vd))
        new_s[2].append(_tokens_major(ktb, (db, ds), ksb))
        new_s[3].append(_tokens_major(vtb, (db, ds), ksb))

    stacked = lambda parts: jnp.stack(parts) if depth > 1 else parts[0][None]
    return (xp.reshape(b, t, d), xs.reshape(db, ds, d),
            *[stacked(parts) for parts in new_p], *[stacked(parts) for parts in new_s])
```

```python
import functools
import math

import jax
import jax.numpy as jnp
from jax import lax
from jax.experimental import pallas as pl
from jax.experimental.pallas import tpu as pltpu

F32 = jnp.float32
BF16 = jnp.bfloat16

H_A = 4
DH_A = 64
DV_A = 2 * DH_A
H_B = 8
DH_B = 64
EPS = 1e-6
NEG_INF = -1e30
LOG2_E = math.log2(math.e)
W_GROUP = H_A * DV_A
N_GROUPS_QKV = 6
TOKEN_MINOR_GROUPS = (1, 4, 5)
MIX = 2 * W_GROUP

LANES = 128
SUBLANES = 8
HALF = LANES // 2
VMEM_LIMIT = 56 * 1024 * 1024

ALIBI_LOG2_STEP = 8 // H_A
assert 8 % H_A == 0

TQ = 512
TK_SB = 256
ROW_CHUNK = 128
STREAMS = 4
DIFF_STREAMS = 2
Q_PAD = SUBLANES
ROWS_S = 2 * H_A * Q_PAD
assert ROWS_S == H_B * Q_PAD
PAGES_PER_STEP = 16


def _pow2_neg(e):
    return lax.bitcast_convert_type((127 - e) << 23, F32)


def _rms(x, g):
    return x * lax.rsqrt(jnp.mean(x * x, axis=-1, keepdims=True) + EPS) * g


def _nt_dot(a, b):
    return lax.dot_general(a, b, (((1,), (1,)), ((), ())), preferred_element_type=F32)


def _dot(a, b):
    return jnp.dot(a, b, preferred_element_type=F32)


def _lambda(lam_ref, lam_init):
    t1 = jnp.sum(lam_ref[0] * lam_ref[1], axis=-1, keepdims=True)
    t2 = jnp.sum(lam_ref[2] * lam_ref[3], axis=-1, keepdims=True)
    return jnp.exp(t1) - jnp.exp(t2) + lam_init


def _softplus(z):
    return jnp.maximum(z, 0.0) + jnp.log(1.0 + jnp.exp2(jnp.abs(z) * (-LOG2_E)))


def _strict_lower_ones(n):
    r = lax.broadcasted_iota(jnp.int32, (n, n), 0)
    c = lax.broadcasted_iota(jnp.int32, (n, n), 1)
    return jnp.where(r > c, 1.0, 0.0).astype(BF16)


def _proj_kernel(x_ref, g_ref, w_ref, zb_ref, kta_ref, va_ref, ktb_ref, vtb_ref, gate_ref):
    h = _rms(x_ref[...], g_ref[...]).astype(BF16)
    tm = h.shape[0]

    def cols(i):
        return _dot(h, w_ref[:, i * W_GROUP:(i + 1) * W_GROUP])

    token_minor = dict(zip(TOKEN_MINOR_GROUPS, (kta_ref, ktb_ref, vtb_ref)))
    for i in range(N_GROUPS_QKV):
        z = cols(i)
        if i == 2:
            for hh in range(H_A):
                va_ref[pl.ds(hh, tm, stride=H_A), :] = z[:, hh * DV_A:(hh + 1) * DV_A]
        if i in token_minor:
            token_minor[i][...] = z.T
        if i in (0, 3):
            z = z * (DH_A ** -0.5)
        zb_ref[:, i * W_GROUP:(i + 1) * W_GROUP] = z.astype(BF16)
    for i in range(2):
        gate_ref[:, i * W_GROUP:(i + 1) * W_GROUP] = cols(N_GROUPS_QKV + i)


def _proj(x, g, w_bf16, tm, rows_per_batch):
    n, d = x.shape
    d_in = w_bf16.shape[1]
    assert n % rows_per_batch == 0 and rows_per_batch % tm == 0
    assert d_in == (N_GROUPS_QKV + 2) * W_GROUP
    tiles_per_batch = rows_per_batch // tm
    row = lambda i: (i, 0)
    fixed = lambda i: (0, 0)
    tok_minor = lambda i: (i // tiles_per_batch, 0, i % tiles_per_batch)
    kt = jax.ShapeDtypeStruct((n // rows_per_batch, W_GROUP, rows_per_batch), F32)
    kt_spec = pl.BlockSpec((None, W_GROUP, tm), tok_minor)
    return pl.pallas_call(
        _proj_kernel,
        grid=(n // tm,),
        in_specs=[pl.BlockSpec((tm, d), row), pl.BlockSpec((1, d), fixed),
                  pl.BlockSpec((d, d_in), fixed)],
        out_specs=[pl.BlockSpec((tm, N_GROUPS_QKV * W_GROUP), row), kt_spec,
                   pl.BlockSpec((tm * H_A, DV_A), row), kt_spec, kt_spec,
                   pl.BlockSpec((tm, MIX), row)],
        out_shape=[jax.ShapeDtypeStruct((n, N_GROUPS_QKV * W_GROUP), BF16), kt,
                   jax.ShapeDtypeStruct((n * H_A, DV_A), F32), kt, kt,
                   jax.ShapeDtypeStruct((n, MIX), F32)],
        compiler_params=pltpu.CompilerParams(dimension_semantics=("parallel",),
                                             vmem_limit_bytes=VMEM_LIMIT),
        name="proj",
    )(x, g.reshape(1, d), w_bf16)


def _stack_halves(x):
    lane = lax.broadcasted_iota(jnp.int32, x.shape, 1)
    zero = jnp.zeros_like(x)
    return jnp.concatenate([jnp.where(lane < HALF, x, zero), jnp.where(lane >= HALF, x, zero)], axis=0)


def _row_chunks():
    return [(r0, r0 + ROW_CHUNK) for r0 in range(0, 2 * TQ, ROW_CHUNK)]


def _causal_iotas(r0, tk):
    r = (lax.broadcasted_iota(jnp.int32, (ROW_CHUNK, tk), 0) + r0) & (TQ - 1)
    c = lax.broadcasted_iota(jnp.int32, (ROW_CHUNK, tk), 1)
    return r, c


def _lane_replicated(x):
    return jnp.broadcast_to(x, (x.shape[0], LANES))


def _lane_tiled(x, width):
    return jnp.concatenate([x] * (width // LANES), axis=1)


def _stream_lanes():
    return [slice(s * LANES, (s + 1) * LANES) for s in range(STREAMS)]


def _run_interleaved(*stage_generators):
    live = list(stage_generators)
    while live:
        for gen in list(live):
            try:
                next(gen)
            except StopIteration:
                live.remove(gen)


def _diff_block_stages(q_ref, k_ref, v_ref, m_sc, acc_sc, head0, kj, key_offset, diag):
    tk = TQ
    lanes = [slice(s * LANES, (s + 1) * LANES) for s in range(DIFF_STREAMS)]
    chunks = _row_chunks()
    qs = [_stack_halves(q_ref[:, ln]) for ln in lanes]
    slope = [_pow2_neg(jnp.full((1, tk), ALIBI_LOG2_STEP, jnp.int32) * (head0 + s + 1))
             for s in range(DIFF_STREAMS)]
    col = lax.broadcasted_iota(jnp.int32, (1, tk), 1).astype(F32)
    ones = jnp.ones((tk, LANES), BF16)
    start = pl.multiple_of(kj * tk, tk)
    s_all = [_nt_dot(qs[s], k_ref[pl.ds(start, tk), ln]) for s, ln in enumerate(lanes)]
    vs = [jnp.concatenate([v_ref[pl.ds(start, tk), ln], ones], axis=1) for ln in lanes]
    yield
    bias = [sl * (col + key_offset) for sl in slope]

    def scores(s, r0, r1):
        x = s_all[s][r0:r1] + bias[s]
        if diag:
            r, c = _causal_iotas(r0, tk)
            x = jnp.where(c <= r, x, NEG_INF)
        return x

    m_new = [jnp.concatenate(
        [jnp.maximum(m_sc[s, r0:r1, :],
                     _lane_replicated(jnp.max(scores(s, r0, r1), axis=-1, keepdims=True)))
         for r0, r1 in chunks], axis=0) for s in range(DIFF_STREAMS)]
    p = [jnp.concatenate(
        [jnp.exp(scores(s, r0, r1) - _lane_tiled(m_new[s][r0:r1], tk)).astype(BF16)
         for r0, r1 in chunks], axis=0) for s in range(DIFF_STREAMS)]
    yield
    for s in range(DIFF_STREAMS):
        alpha = jnp.exp(m_sc[s] - m_new[s])
        acc_sc[s] = _lane_tiled(alpha, 2 * LANES) * acc_sc[s] + _dot(p[s], vs[s])
        m_sc[s] = m_new[s]


def _diff_finalize(acc_sc, lam_ref, g_ref, o_ref, lam_init):
    lam = _lambda(lam_ref, lam_init)
    for s in range(DIFF_STREAMS):
        acc = acc_sc[s]
        o = acc[:, :LANES] / acc[:, LANES:]
        d = o[:TQ] - lam * o[TQ:]
        o_ref[:, s * LANES:(s + 1) * LANES] = _rms(d, g_ref[...]) * (1.0 - lam_init)


def _sb_prompt_kernel(q_ref, k_ref, v_ref, o_ref, run_sc, acc_sc):
    tk = TK_SB
    n_sub = TQ // tk
    qi = pl.program_id(2)
    lanes = _stream_lanes()
    chunks = _row_chunks()
    half = len(chunks) // 2
    qs = [_stack_halves(q_ref[:, ln]) for ln in lanes]
    tri = _strict_lower_ones(tk)
    run_sc[...] = jnp.zeros(run_sc.shape, F32)
    acc_sc[...] = jnp.zeros(acc_sc.shape, F32)

    def block(kj, diag_sub=None):
        start = pl.multiple_of(kj * tk, tk)
        z_all = [_nt_dot(qs[s], k_ref[pl.ds(start, tk), ln]) for s, ln in enumerate(lanes)]
        vs = [_stack_halves(v_ref[pl.ds(start, tk), ln]) for ln in lanes]
        masks = [None] * len(chunks)
        if diag_sub is not None:
            for i, (r0, _) in enumerate(chunks):
                r, c = _causal_iotas(r0, tk)
                masks[i] = c + diag_sub * tk < r
        log_beta, drops, sums = [], [], []
        for s in range(STREAMS):
            lb_s, drop_s, sum_s = [], [], []
            for (r0, r1), mask in zip(chunks, masks):
                z = z_all[s][r0:r1]
                drop = _softplus(z)
                lb_s.append(z - drop)
                if mask is not None:
                    drop = jnp.where(mask, drop, 0.0)
                drop_s.append(drop.astype(BF16))
                sum_s.append(_lane_replicated(jnp.sum(drop, axis=-1, keepdims=True)))
            log_beta.append(lb_s)
            drops.append(jnp.concatenate(drop_s, axis=0))
            sums.append(jnp.concatenate(sum_s, axis=0))
        after = [_dot(drops[s], tri) for s in range(STREAMS)]
        for s in range(STREAMS):
            w_out = []
            for (r0, r1), mask, lb in zip(chunks, masks, log_beta[s]):
                w = jnp.exp(lb - after[s][r0:r1] - _lane_tiled(run_sc[s, r0:r1, :], tk))
                if mask is not None:
                    w = jnp.where(mask, w, 0.0)
                w_out.append(w.astype(BF16))
            wb = jnp.concatenate([jnp.concatenate(w_out[:half], axis=0),
                                  jnp.concatenate(w_out[half:], axis=0)], axis=1)
            run_sc[s] = run_sc[s] + sums[s]
            acc_sc[s] = acc_sc[s] + _dot(wb, vs[s])

    def full_block(i, _):
        block(qi * n_sub - 1 - i)
        return 0

    for sub in reversed(range(n_sub)):
        block(qi * n_sub + sub, diag_sub=sub)
    lax.fori_loop(0, qi * n_sub, full_block, 0)
    for s, ln in enumerate(lanes):
        o_ref[:, ln] = acc_sc[s]


def _prompt_attention(body, zb, extra, q_blk, k_blk, v_blk, scratch, name):
    b, t, _ = zb.shape
    assert t % TQ == 0 and TQ % TK_SB == 0
    width = STREAMS * LANES
    assert W_GROUP % width == 0
    n_grp = W_GROUP // width
    extra_specs = [pl.BlockSpec(e.shape, lambda bi, g, qi, nd=e.ndim: (0,) * nd) for e in extra]
    return pl.pallas_call(
        body,
        grid=(b, n_grp, t // TQ),
        in_specs=[pl.BlockSpec((None, TQ, width), lambda bi, g, qi: (bi, qi, q_blk * n_grp + g)),
                  pl.BlockSpec((None, t, width), lambda bi, g, qi: (bi, 0, k_blk * n_grp + g)),
                  pl.BlockSpec((None, t, width), lambda bi, g, qi: (bi, 0, v_blk * n_grp + g))]
        + extra_specs,
        out_specs=pl.BlockSpec((None, TQ, width), lambda bi, g, qi: (bi, qi, g)),
        out_shape=jax.ShapeDtypeStruct((b, t, W_GROUP), F32),
        scratch_shapes=scratch,
        compiler_params=pltpu.CompilerParams(
            dimension_semantics=("parallel", "parallel", "parallel"),
            vmem_limit_bytes=VMEM_LIMIT),
        name=name,
    )(zb, zb, zb, *extra)


def _diff_sample_kernel(pt_ref, qtab_ref, ktab_ref, q_ref, k_ref, v_ref, lam_ref, g_ref,
                        qa_ref, qb_ref, kna_ref, vna_ref, knb_ref, vnb_ref, *rest,
                        past, page, lam_init, blocks_per_group, n_groups, sample_steps):
    del pt_ref
    n_cached = 4 * PAGES_PER_STEP
    pools = [rest[i * PAGES_PER_STEP:(i + 1) * PAGES_PER_STEP] for i in range(4)]
    (o_ref, oa_ref, ob_ref, dm_sc, dacc_sc,
     m_sc, l_sc, acca_sc, run_sc, accb_sc) = rest[n_cached:]
    t = pl.program_id(0)
    n_pages = past // page
    steps_per_row = n_pages // PAGES_PER_STEP + 1
    slot = t % blocks_per_group
    qi = qtab_ref[slot]
    kb = ktab_ref[slot]
    head0 = ((t // blocks_per_group) % n_groups) * DIFF_STREAMS
    j = t % steps_per_row
    sample_live = t < sample_steps
    rows_h = 2 * Q_PAD
    r = lax.broadcasted_iota(jnp.int32, (ROWS_S, page), 0)
    c = lax.broadcasted_iota(jnp.int32, (ROWS_S, page), 1)
    col = c.astype(F32)
    q_of_row = r & (Q_PAD - 1)
    slope = _pow2_neg(ALIBI_LOG2_STEP * ((r // rows_h) + 1))
    tri = _strict_lower_ones(page)

    def sample_stages(blocks, masked):
        qa = qa_ref[...]
        qb = qb_ref[...]
        scores = [_dot(qa, blk[0][...].astype(BF16)) for blk in blocks]
        logits = [_dot(qb, blk[2][...].astype(BF16)) for blk in blocks]
        yield

        for i, blk in enumerate(blocks):
            scores[i] = scores[i] + slope * (col + blk[4])
            if masked:
                scores[i] = jnp.where(c <= q_of_row, scores[i], NEG_INF)
        m = m_sc[...]
        m_new = jnp.maximum(m, jnp.max(functools.reduce(jnp.maximum, scores), axis=-1, keepdims=True))
        alpha = jnp.exp(m - m_new)
        probs = [jnp.exp(s - m_new) for s in scores]
        m_sc[...] = m_new
        l_sc[...] = alpha * l_sc[...] + jnp.sum(functools.reduce(jnp.add, probs), axis=-1,
                                                keepdims=True)
        probs = [p.astype(BF16) for p in probs]

        mask = (c < q_of_row) if masked else None
        log_beta, drops, sums = [], [], []
        for z in logits:
            drop = _softplus(z)
            log_beta.append(z - drop)
            if masked:
                drop = jnp.where(mask, drop, 0.0)
            drops.append(drop.astype(BF16))
            sums.append(jnp.sum(drop, axis=-1, keepdims=True))
        after = [_dot(d, tri) for d in drops]
        yield

        for h in range(H_A):
            rows = slice(h * rows_h, (h + 1) * rows_h)
            upd = None
            for blk, p in zip(blocks, probs):
                v_h = blk[1][pl.ds(h, page, stride=H_A), :].astype(BF16)
                d = _dot(p[rows], v_h)
                upd = d if upd is None else upd + d
            acca_sc[rows, :] = alpha[rows] * acca_sc[rows, :] + upd

        run = run_sc[...]
        weights = []
        for lb, aft, rs in zip(log_beta, after, sums):
            w = jnp.exp(lb - aft - run)
            if masked:
                w = jnp.where(mask, w, 0.0)
            weights.append(w.astype(BF16))
            run = run + rs
        run_sc[...] = run
        upd = None
        for blk, w in zip(blocks, weights):
            d = _nt_dot(w, blk[3][...].astype(BF16))
            upd = d if upd is None else upd + d
        accb_sc[...] = accb_sc[...] + upd

    def page_stages():
        newest = n_pages - 1 - (j - 1) * PAGES_PER_STEP
        return sample_stages(
            [(pools[0][i], pools[1][i], pools[2][i], pools[3][i],
              ((newest - i) * page - past).astype(F32)) for i in range(PAGES_PER_STEP)], False)

    def prompt_stages(diag):
        return _diff_block_stages(q_ref, k_ref, v_ref, dm_sc, dacc_sc, head0, kb,
                                  ((kb - qi) * TQ).astype(F32), diag)

    is_diag = kb == qi
    has_pages = jnp.logical_and(sample_live, j > 0)

    @pl.when(kb == 0)
    def _():
        dm_sc[...] = jnp.full(dm_sc.shape, NEG_INF, F32)
        dacc_sc[...] = jnp.zeros(dacc_sc.shape, F32)

    @pl.when(jnp.logical_and(sample_live, j == 0))
    def _():
        m_sc[...] = jnp.full(m_sc.shape, NEG_INF, F32)
        l_sc[...] = jnp.zeros(l_sc.shape, F32)
        acca_sc[...] = jnp.zeros(acca_sc.shape, F32)
        run_sc[...] = jnp.zeros(run_sc.shape, F32)
        accb_sc[...] = jnp.zeros(accb_sc.shape, F32)
        _run_interleaved(sample_stages([(kna_ref, vna_ref, knb_ref, vnb_ref, 0.0)], True))

    @pl.when(jnp.logical_and(jnp.logical_not(is_diag), has_pages))
    def _():
        _run_interleaved(prompt_stages(False), page_stages())

    @pl.when(jnp.logical_and(jnp.logical_not(is_diag), jnp.logical_not(has_pages)))
    def _():
        _run_interleaved(prompt_stages(False))

    @pl.when(is_diag)
    def _():
        _run_interleaved(prompt_stages(True))
        _diff_finalize(dacc_sc, lam_ref, g_ref, o_ref, lam_init)

    @pl.when(jnp.logical_and(is_diag, has_pages))
    def _():
        _run_interleaved(page_stages())

    @pl.when(jnp.logical_and(sample_live, j == steps_per_row - 1))
    def _():
        o = acca_sc[...] / l_sc[...]
        lam = _lambda(lam_ref, lam_init)
        for h in range(H_A):
            d = (o[h * rows_h:h * rows_h + Q_PAD]
                 - lam * o[h * rows_h + Q_PAD:(h + 1) * rows_h])
            oa_ref[:, h * DV_A:(h + 1) * DV_A] = _rms(d, g_ref[...]) * (1.0 - lam_init)
        acc = accb_sc[...]
        lane_head = lax.broadcasted_iota(jnp.int32, (Q_PAD, W_GROUP), 1) // DH_B
        out = jnp.zeros((Q_PAD, W_GROUP), F32)
        for h in range(H_B):
            out = out + jnp.where(lane_head == h, acc[h * Q_PAD:(h + 1) * Q_PAD], 0.0)
        ob_ref[...] = out


def _diff_and_sample_attention(zb, page_table, qbd_a, qbd_b, kn_a, vn_a, kn_b, vn_b,
                               pool_arrays, lam_pack, g_subln, pool_base, lam_init):
    b, t, _ = zb.shape
    db, n_pages = page_table.shape
    page = pool_arrays[0].shape[2]
    past = n_pages * page
    assert n_pages % PAGES_PER_STEP == 0 and t % TQ == 0
    width = DIFF_STREAMS * LANES
    n_groups = W_GROUP // width
    nq = t // TQ
    pairs = [(qi, kb) for qi in range(nq) for kb in range(qi + 1)]
    qtab = jnp.array([p[0] for p in pairs], jnp.int32)
    ktab = jnp.array([p[1] for p in pairs], jnp.int32)
    bpg = len(pairs)
    n_steps = b * n_groups * bpg
    steps_per_row = n_pages // PAGES_PER_STEP + 1
    sample_steps = db * steps_per_row
    assert sample_steps <= n_steps, "the prompt sweep must have a step for every page group"

    def prompt_map(col_group):
        def index(s, pt, qt, kt):
            bg = s // bpg
            return (bg // n_groups, 0, col_group * n_groups + bg % n_groups)
        return index

    def q_index(s, pt, qt, kt):
        bg = s // bpg
        return (bg // n_groups, qt[s % bpg], bg % n_groups)

    def row_index(s, pt, qt, kt):
        return (jnp.minimum(s, sample_steps - 1) // steps_per_row, 0, 0)

    def paged(i):
        def index(s, pt, qt, kt):
            s = jnp.minimum(s, sample_steps - 1)
            j = s % steps_per_row
            newest = n_pages - 1 - (jnp.maximum(j, 1) - 1) * PAGES_PER_STEP
            return (pool_base + pt[s // steps_per_row, newest - i], 0, 0)
        return pl.BlockSpec((None, W_GROUP, page), index)

    fixed = lambda nd: (lambda s, pt, qt, kt: (0,) * nd)
    new_spec = pl.BlockSpec((None, W_GROUP, page), row_index)
    q_spec = pl.BlockSpec((None, ROWS_S, W_GROUP), row_index)
    out_spec = pl.BlockSpec((None, Q_PAD, W_GROUP), row_index)
    out = jax.ShapeDtypeStruct((db, Q_PAD, W_GROUP), F32)
    cache_specs = [paged(i) for _ in pool_arrays for i in range(PAGES_PER_STEP)]
    cache_args = [a for a in pool_arrays for _ in range(PAGES_PER_STEP)]
    return pl.pallas_call(
        functools.partial(_diff_sample_kernel, past=past, page=page, lam_init=lam_init,
                          blocks_per_group=bpg, n_groups=n_groups, sample_steps=sample_steps),
        grid_spec=pltpu.PrefetchScalarGridSpec(
            num_scalar_prefetch=3,
            grid=(n_steps,),
            in_specs=[pl.BlockSpec((None, TQ, width), q_index),
                      pl.BlockSpec((None, t, width), prompt_map(1)),
                      pl.BlockSpec((None, t, width), prompt_map(2)),
                      pl.BlockSpec(lam_pack.shape, fixed(3)), pl.BlockSpec(g_subln.shape, fixed(2)),
                      q_spec, q_spec, new_spec, new_spec, new_spec, new_spec] + cache_specs,
            out_specs=[pl.BlockSpec((None, TQ, width), q_index), out_spec, out_spec],
            scratch_shapes=[pltpu.VMEM((DIFF_STREAMS, 2 * TQ, LANES), F32),
                            pltpu.VMEM((DIFF_STREAMS, 2 * TQ, 2 * LANES), F32),
                            pltpu.VMEM((ROWS_S, 1), F32), pltpu.VMEM((ROWS_S, 1), F32),
                            pltpu.VMEM((ROWS_S, DV_A), F32), pltpu.VMEM((ROWS_S, 1), F32),
                            pltpu.VMEM((ROWS_S, W_GROUP), F32)]),
        out_shape=[jax.ShapeDtypeStruct((b, t, W_GROUP), F32), out, out],
        compiler_params=pltpu.CompilerParams(dimension_semantics=("arbitrary",),
                                             vmem_limit_bytes=VMEM_LIMIT),
        name="diff_sample_attention",
    )(page_table, qtab, ktab, zb, zb, zb, lam_pack, g_subln,
      qbd_a, qbd_b, kn_a, vn_a, kn_b, vn_b, *cache_args)


def _sigmoid(x):
    return 1.0 / (1.0 + jnp.exp(-x))


def _merge_kernel(x_ref, oa_ref, ob_ref, gate_ref, p_ref, wo_ref, gp_ref, wg_ref, wp_ref, gf_ref,
                  y_ref, *, final):
    gate = gate_ref[...]
    sg = gate * _sigmoid(gate)
    ua = (oa_ref[...] * sg[:, :W_GROUP]).astype(BF16)
    ub = (ob_ref[...] * sg[:, W_GROUP:]).astype(BF16)
    x1 = x_ref[...] + _dot(ua, wo_ref[:W_GROUP, :]) + _dot(ub, wo_ref[W_GROUP:, :])
    hn = _rms(x1, gp_ref[...]).astype(BF16)
    g = _sigmoid(_dot(hn, wg_ref[...]))
    pe = _dot(p_ref[...].astype(BF16), wp_ref[...])
    x2 = x1 + g * pe
    y_ref[...] = _rms(x2, gf_ref[...]) if final else x2


def _merge(x, o_a, o_b, gate, p, w_out, g_ple, w_gate, w_proj, g_final, tm, final):
    n, d = x.shape
    assert n % tm == 0
    row = lambda i: (i, 0)
    fixed = lambda i: (0, 0)
    full = lambda a: pl.BlockSpec(a.shape, fixed)
    g_ple = g_ple.reshape(1, d)
    g_final = g_final.reshape(1, d)
    return pl.pallas_call(
        functools.partial(_merge_kernel, final=final),
        grid=(n // tm,),
        in_specs=[pl.BlockSpec((tm, d), row), pl.BlockSpec((tm, W_GROUP), row),
                  pl.BlockSpec((tm, W_GROUP), row), pl.BlockSpec((tm, MIX), row),
                  pl.BlockSpec((tm, p.shape[1]), row), full(w_out), full(g_ple), full(w_gate),
                  full(w_proj), full(g_final)],
        out_specs=pl.BlockSpec((tm, d), row),
        out_shape=jax.ShapeDtypeStruct((n, d), F32),
        compiler_params=pltpu.CompilerParams(dimension_semantics=("parallel",),
                                             vmem_limit_bytes=VMEM_LIMIT),
        name="merge",
    )(x, o_a, o_b, gate, p, w_out, g_ple, w_gate, w_proj, g_final)


def _block_diag_queries(q, group_width):
    n_groups = W_GROUP // group_width
    lane_group = jnp.arange(W_GROUP, dtype=jnp.int32) // group_width
    mask = (lane_group[None, :] == jnp.arange(n_groups, dtype=jnp.int32)[:, None]).astype(q.dtype)
    out = q[:, None, :, :] * mask[None, :, None, :]
    return out.reshape(q.shape[0], n_groups * Q_PAD, W_GROUP)


def _tokens_major(kt, lead, tail):
    nb, _, rows = kt.shape
    a = jnp.moveaxis(kt.reshape(nb, *tail, rows), -1, 1)
    return a.reshape(*lead, *tail)


def kernel(x_prompt, x_sample, cache_k_diff, cache_v_diff, cache_k_sb, cache_v_sb, page_table,
           p_prompt, p_sample, g_norm, w_in, lambda_q1, lambda_k1, lambda_q2, lambda_k2,
           g_subln, w_out, g_ple, w_ple_gate, w_ple_proj, g_final):
    b, t, d = x_prompt.shape
    db, ds, _ = x_sample.shape
    depth, n_pool, page = cache_k_diff.shape[:3]
    assert ds <= Q_PAD and page * H_A == W_GROUP and DV_A == page

    xp = x_prompt.reshape(b * t, d)
    xs = x_sample.reshape(db * ds, d)
    n_all = depth * n_pool
    pools = [cache_k_diff.transpose(0, 1, 3, 4, 5, 2).reshape(n_all, W_GROUP, page),
             cache_v_diff.reshape(n_all, page * H_A, DV_A),
             cache_k_sb.transpose(0, 1, 3, 4, 2).reshape(n_all, W_GROUP, page),
             cache_v_sb.transpose(0, 1, 3, 4, 2).reshape(n_all, W_GROUP, page)]

    kd, vd, ksb = (H_A, 2, DH_A), (H_A, DV_A), (H_B, DH_B)
    new_p = [[] for _ in range(4)]
    new_s = [[] for _ in range(4)]
    for l in range(depth):
        lam_init = 0.8 - 0.6 * math.exp(-0.3 * l)
        final = l == depth - 1
        w_in_l = w_in[l].astype(BF16)
        w_out_l = w_out[l].astype(BF16)
        w_gate_l = w_ple_gate[l].astype(BF16)
        w_proj_l = w_ple_proj[l].astype(BF16)
        lam_pack = jnp.stack([lambda_q1[l], lambda_k1[l], lambda_q2[l], lambda_k2[l]])[:, None, :]
        g_sub = g_subln[l].reshape(1, DV_A)

        zb, kta, va, ktb, vtb, gate = _proj(xp, g_norm[l], w_in_l, 512, t)
        zb3 = zb.reshape(b, t, N_GROUPS_QKV * W_GROUP)
        new_p[0].append(_tokens_major(kta, (b, t), kd))
        new_p[1].append(va.reshape(b, t, *vd))
        new_p[2].append(_tokens_major(ktb, (b, t), ksb))
        new_p[3].append(_tokens_major(vtb, (b, t), ksb))
        gate_p = gate
        zb, kta, va, ktb, vtb, gate = _proj(xs, g_norm[l], w_in_l, db * ds, db * ds)
        zs = zb.reshape(db, ds, N_GROUPS_QKV, W_GROUP)
        q_pad = lambda i: jnp.pad(zs[:, :, i], ((0, 0), (0, Q_PAD - ds), (0, 0)))
        per_b = lambda kt: jnp.pad(kt.reshape(W_GROUP, db, ds).transpose(1, 0, 2),
                                   ((0, 0), (0, 0), (0, page - ds)))
        vn_a = jnp.pad(va.reshape(db, ds, H_A, DV_A), ((0, 0), (0, page - ds), (0, 0), (0, 0)))

        o_a, so_a, so_b = _diff_and_sample_attention(
            zb3, page_table, _block_diag_queries(q_pad(0), DH_A),
            _block_diag_queries(q_pad(3), DH_B),
            per_b(kta), vn_a.reshape(db, page * H_A, DV_A), per_b(ktb), per_b(vtb),
            pools, lam_pack, g_sub, l * n_pool, lam_init)
        o_b = _prompt_attention(_sb_prompt_kernel, zb3, [], 3, 4, 5,
                                [pltpu.VMEM((STREAMS, 2 * TQ, LANES), F32),
                                 pltpu.VMEM((STREAMS, TQ, LANES), F32)], "sb_prompt")

        xp = _merge(xp, o_a.reshape(b * t, W_GROUP), o_b.reshape(b * t, W_GROUP), gate_p,
                    p_prompt[l].reshape(b * t, -1), w_out_l, g_ple[l], w_gate_l, w_proj_l,
                    g_final, 512, final)
        xs = _merge(xs, so_a[:, :ds].reshape(db * ds, W_GROUP), so_b[:, :ds].reshape(db * ds, W_GROUP),
                    gate, p_sample[l].reshape(db * ds, -1), w_out_l, g_ple[l], w_gate_l, w_proj_l,
                    g_final, db * ds, final)
        new_s[0].append(_tokens_major(kta, (db, ds), kd))
        new_s[1].append(va.reshape(db, ds, *vd))
        new_s[2].append(_tokens_major(ktb, (db, ds), ksb))
        new_s[3].append(_tokens_major(vtb, (db, ds), ksb))

    stacked = lambda parts: jnp.stack(parts) if depth > 1 else parts[0][None]
    return (xp.reshape(b, t, d), xs.reshape(db, ds, d),
            *[stacked(parts) for parts in new_p], *[stacked(parts) for parts in new_s])
```

```python
import functools
import math

import jax
import jax.numpy as jnp
from jax import lax
from jax.experimental import pallas as pl
from jax.experimental.pallas import tpu as pltpu

F32 = jnp.float32
BF16 = jnp.bfloat16

H_A = 4
DH_A = 64
DV_A = 2 * DH_A
H_B = 8
DH_B = 64
EPS = 1e-6
NEG_INF = -1e30
LOG2_E = math.log2(math.e)
W_GROUP = H_A * DV_A
N_GROUPS_QKV = 6
TOKEN_MINOR_GROUPS = (1, 4, 5)
MIX = 2 * W_GROUP

LANES = 128
SUBLANES = 8
HALF = LANES // 2
VMEM_LIMIT = 56 * 1024 * 1024

ALIBI_LOG2_STEP = 8 // H_A
assert 8 % H_A == 0

TQ = 512
TK_SB = 256
ROW_CHUNK = 128
STREAMS = 4
DIFF_STREAMS = 2
Q_PAD = SUBLANES
ROWS_S = 2 * H_A * Q_PAD
assert ROWS_S == H_B * Q_PAD
PAGES_PER_STEP = 8


def _pow2_neg(e):
    return lax.bitcast_convert_type((127 - e) << 23, F32)


def _rms(x, g):
    return x * lax.rsqrt(jnp.mean(x * x, axis=-1, keepdims=True) + EPS) * g


def _nt_dot(a, b):
    return lax.dot_general(a, b, (((1,), (1,)), ((), ())), preferred_element_type=F32)


def _dot(a, b):
    return jnp.dot(a, b, preferred_element_type=F32)


def _lambda(lam_ref, lam_init):
    t1 = jnp.sum(lam_ref[0] * lam_ref[1], axis=-1, keepdims=True)
    t2 = jnp.sum(lam_ref[2] * lam_ref[3], axis=-1, keepdims=True)
    return jnp.exp(t1) - jnp.exp(t2) + lam_init


def _softplus(z):
    return jnp.maximum(z, 0.0) + jnp.log(1.0 + jnp.exp2(jnp.abs(z) * (-LOG2_E)))


def _strict_lower_ones(n):
    r = lax.broadcasted_iota(jnp.int32, (n, n), 0)
    c = lax.broadcasted_iota(jnp.int32, (n, n), 1)
    return jnp.where(r > c, 1.0, 0.0).astype(BF16)


def _proj_kernel(x_ref, g_ref, w_ref, zb_ref, kta_ref, va_ref, ktb_ref, vtb_ref, gate_ref):
    h = _rms(x_ref[...], g_ref[...]).astype(BF16)
    tm = h.shape[0]

    def cols(i):
        return _dot(h, w_ref[:, i * W_GROUP:(i + 1) * W_GROUP])

    token_minor = dict(zip(TOKEN_MINOR_GROUPS, (kta_ref, ktb_ref, vtb_ref)))
    for i in range(N_GROUPS_QKV):
        z = cols(i)
        if i == 2:
            for hh in range(H_A):
                va_ref[pl.ds(hh, tm, stride=H_A), :] = z[:, hh * DV_A:(hh + 1) * DV_A]
        if i in token_minor:
            token_minor[i][...] = z.T
        if i in (0, 3):
            z = z * (DH_A ** -0.5)
        zb_ref[:, i * W_GROUP:(i + 1) * W_GROUP] = z.astype(BF16)
    for i in range(2):
        gate_ref[:, i * W_GROUP:(i + 1) * W_GROUP] = cols(N_GROUPS_QKV + i)


def _proj(x, g, w_bf16, tm, rows_per_batch):
    n, d = x.shape
    d_in = w_bf16.shape[1]
    assert n % rows_per_batch == 0 and rows_per_batch % tm == 0
    assert d_in == (N_GROUPS_QKV + 2) * W_GROUP
    tiles_per_batch = rows_per_batch // tm
    row = lambda i: (i, 0)
    fixed = lambda i: (0, 0)
    tok_minor = lambda i: (i // tiles_per_batch, 0, i % tiles_per_batch)
    kt = jax.ShapeDtypeStruct((n // rows_per_batch, W_GROUP, rows_per_batch), F32)
    kt_spec = pl.BlockSpec((None, W_GROUP, tm), tok_minor)
    return pl.pallas_call(
        _proj_kernel,
        grid=(n // tm,),
        in_specs=[pl.BlockSpec((tm, d), row), pl.BlockSpec((1, d), fixed),
                  pl.BlockSpec((d, d_in), fixed)],
        out_specs=[pl.BlockSpec((tm, N_GROUPS_QKV * W_GROUP), row), kt_spec,
                   pl.BlockSpec((tm * H_A, DV_A), row), kt_spec, kt_spec,
                   pl.BlockSpec((tm, MIX), row)],
        out_shape=[jax.ShapeDtypeStruct((n, N_GROUPS_QKV * W_GROUP), BF16), kt,
                   jax.ShapeDtypeStruct((n * H_A, DV_A), F32), kt, kt,
                   jax.ShapeDtypeStruct((n, MIX), F32)],
        compiler_params=pltpu.CompilerParams(dimension_semantics=("parallel",),
                                             vmem_limit_bytes=VMEM_LIMIT),
        name="proj",
    )(x, g.reshape(1, d), w_bf16)


def _stack_halves(x):
    lane = lax.broadcasted_iota(jnp.int32, x.shape, 1)
    zero = jnp.zeros_like(x)
    return jnp.concatenate([jnp.where(lane < HALF, x, zero), jnp.where(lane >= HALF, x, zero)], axis=0)


def _row_chunks():
    return [(r0, r0 + ROW_CHUNK) for r0 in range(0, 2 * TQ, ROW_CHUNK)]


def _causal_iotas(r0, tk):
    r = (lax.broadcasted_iota(jnp.int32, (ROW_CHUNK, tk), 0) + r0) & (TQ - 1)
    c = lax.broadcasted_iota(jnp.int32, (ROW_CHUNK, tk), 1)
    return r, c


def _lane_replicated(x):
    return jnp.broadcast_to(x, (x.shape[0], LANES))


def _lane_tiled(x, width):
    return jnp.concatenate([x] * (width // LANES), axis=1)


def _stream_lanes():
    return [slice(s * LANES, (s + 1) * LANES) for s in range(STREAMS)]


def _run_interleaved(*stage_generators):
    live = list(stage_generators)
    while live:
        for gen in list(live):
            try:
                next(gen)
            except StopIteration:
                live.remove(gen)


def _diff_block_stages(q_ref, k_ref, v_ref, m_sc, acc_sc, head0, kj, key_offset, diag):
    tk = TQ
    lanes = [slice(s * LANES, (s + 1) * LANES) for s in range(DIFF_STREAMS)]
    chunks = _row_chunks()
    qs = [_stack_halves(q_ref[:, ln]) for ln in lanes]
    slope = [_pow2_neg(jnp.full((1, tk), ALIBI_LOG2_STEP, jnp.int32) * (head0 + s + 1))
             for s in range(DIFF_STREAMS)]
    col = lax.broadcasted_iota(jnp.int32, (1, tk), 1).astype(F32)
    ones = jnp.ones((tk, LANES), BF16)
    start = pl.multiple_of(kj * tk, tk)
    s_all = [_nt_dot(qs[s], k_ref[pl.ds(start, tk), ln]) for s, ln in enumerate(lanes)]
    vs = [jnp.concatenate([v_ref[pl.ds(start, tk), ln], ones], axis=1) for ln in lanes]
    yield
    bias = [sl * (col + key_offset) for sl in slope]

    def scores(s, r0, r1):
        x = s_all[s][r0:r1] + bias[s]
        if diag:
            r, c = _causal_iotas(r0, tk)
            x = jnp.where(c <= r, x, NEG_INF)
        return x

    m_new = [jnp.concatenate(
        [jnp.maximum(m_sc[s, r0:r1, :],
                     _lane_replicated(jnp.max(scores(s, r0, r1), axis=-1, keepdims=True)))
         for r0, r1 in chunks], axis=0) for s in range(DIFF_STREAMS)]
    p = [jnp.concatenate(
        [jnp.exp(scores(s, r0, r1) - _lane_tiled(m_new[s][r0:r1], tk)).astype(BF16)
         for r0, r1 in chunks], axis=0) for s in range(DIFF_STREAMS)]
    yield
    for s in range(DIFF_STREAMS):
        alpha = jnp.exp(m_sc[s] - m_new[s])
        acc_sc[s] = _lane_tiled(alpha, 2 * LANES) * acc_sc[s] + _dot(p[s], vs[s])
        m_sc[s] = m_new[s]


def _diff_finalize(acc_sc, lam_ref, g_ref, o_ref, lam_init):
    lam = _lambda(lam_ref, lam_init)
    for s in range(DIFF_STREAMS):
        acc = acc_sc[s]
        o = acc[:, :LANES] / acc[:, LANES:]
        d = o[:TQ] - lam * o[TQ:]
        o_ref[:, s * LANES:(s + 1) * LANES] = _rms(d, g_ref[...]) * (1.0 - lam_init)


def _sb_prompt_kernel(q_ref, k_ref, v_ref, o_ref, run_sc, acc_sc):
    tk = TK_SB
    n_sub = TQ // tk
    qi = pl.program_id(2)
    lanes = _stream_lanes()
    chunks = _row_chunks()
    half = len(chunks) // 2
    qs = [_stack_halves(q_ref[:, ln]) for ln in lanes]
    tri = _strict_lower_ones(tk)
    run_sc[...] = jnp.zeros(run_sc.shape, F32)
    acc_sc[...] = jnp.zeros(acc_sc.shape, F32)

    def block(kj, diag_sub=None):
        start = pl.multiple_of(kj * tk, tk)
        z_all = [_nt_dot(qs[s], k_ref[pl.ds(start, tk), ln]) for s, ln in enumerate(lanes)]
        vs = [_stack_halves(v_ref[pl.ds(start, tk), ln]) for ln in lanes]
        masks = [None] * len(chunks)
        if diag_sub is not None:
            for i, (r0, _) in enumerate(chunks):
                r, c = _causal_iotas(r0, tk)
                masks[i] = c + diag_sub * tk < r
        log_beta, drops, sums = [], [], []
        for s in range(STREAMS):
            lb_s, drop_s, sum_s = [], [], []
            for (r0, r1), mask in zip(chunks, masks):
                z = z_all[s][r0:r1]
                drop = _softplus(z)
                lb_s.append(z - drop)
                if mask is not None:
                    drop = jnp.where(mask, drop, 0.0)
                drop_s.append(drop.astype(BF16))
                sum_s.append(_lane_replicated(jnp.sum(drop, axis=-1, keepdims=True)))
            log_beta.append(lb_s)
            drops.append(jnp.concatenate(drop_s, axis=0))
            sums.append(jnp.concatenate(sum_s, axis=0))
        after = [_dot(drops[s], tri) for s in range(STREAMS)]
        for s in range(STREAMS):
            w_out = []
            for (r0, r1), mask, lb in zip(chunks, masks, log_beta[s]):
                w = jnp.exp(lb - after[s][r0:r1] - _lane_tiled(run_sc[s, r0:r1, :], tk))
                if mask is not None:
                    w = jnp.where(mask, w, 0.0)
                w_out.append(w.astype(BF16))
            wb = jnp.concatenate([jnp.concatenate(w_out[:half], axis=0),
                                  jnp.concatenate(w_out[half:], axis=0)], axis=1)
            run_sc[s] = run_sc[s] + sums[s]
            acc_sc[s] = acc_sc[s] + _dot(wb, vs[s])

    def full_block(i, _):
        block(qi * n_sub - 1 - i)
        return 0

    for sub in reversed(range(n_sub)):
        block(qi * n_sub + sub, diag_sub=sub)
    lax.fori_loop(0, qi * n_sub, full_block, 0)
    for s, ln in enumerate(lanes):
        o_ref[:, ln] = acc_sc[s]


def _prompt_attention(body, zb, extra, q_blk, k_blk, v_blk, scratch, name):
    b, t, _ = zb.shape
    assert t % TQ == 0 and TQ % TK_SB == 0
    width = STREAMS * LANES
    assert W_GROUP % width == 0
    n_grp = W_GROUP // width
    extra_specs = [pl.BlockSpec(e.shape, lambda bi, g, qi, nd=e.ndim: (0,) * nd) for e in extra]
    return pl.pallas_call(
        body,
        grid=(b, n_grp, t // TQ),
        in_specs=[pl.BlockSpec((None, TQ, width), lambda bi, g, qi: (bi, qi, q_blk * n_grp + g)),
                  pl.BlockSpec((None, t, width), lambda bi, g, qi: (bi, 0, k_blk * n_grp + g)),
                  pl.BlockSpec((None, t, width), lambda bi, g, qi: (bi, 0, v_blk * n_grp + g))]
        + extra_specs,
        out_specs=pl.BlockSpec((None, TQ, width), lambda bi, g, qi: (bi, qi, g)),
        out_shape=jax.ShapeDtypeStruct((b, t, W_GROUP), F32),
        scratch_shapes=scratch,
        compiler_params=pltpu.CompilerParams(
            dimension_semantics=("parallel", "parallel", "parallel"),
            vmem_limit_bytes=VMEM_LIMIT),
        name=name,
    )(zb, zb, zb, *extra)


def _diff_sample_kernel(pt_ref, qtab_ref, ktab_ref, q_ref, k_ref, v_ref, lam_ref, g_ref,
                        qa_ref, qb_ref, kna_ref, vna_ref, knb_ref, vnb_ref, *rest,
                        past, page, lam_init, blocks_per_group, n_groups, sample_steps):
    del pt_ref
    n_cached = 4 * PAGES_PER_STEP
    pools = [rest[i * PAGES_PER_STEP:(i + 1) * PAGES_PER_STEP] for i in range(4)]
    (o_ref, oa_ref, ob_ref, dm_sc, dacc_sc,
     m_sc, l_sc, acca_sc, run_sc, accb_sc) = rest[n_cached:]
    t = pl.program_id(0)
    n_pages = past // page
    steps_per_row = n_pages // PAGES_PER_STEP + 1
    slot = t % blocks_per_group
    qi = qtab_ref[slot]
    kb = ktab_ref[slot]
    head0 = ((t // blocks_per_group) % n_groups) * DIFF_STREAMS
    j = t % steps_per_row
    sample_live = t < sample_steps
    rows_h = 2 * Q_PAD
    r = lax.broadcasted_iota(jnp.int32, (ROWS_S, page), 0)
    c = lax.broadcasted_iota(jnp.int32, (ROWS_S, page), 1)
    col = c.astype(F32)
    q_of_row = r & (Q_PAD - 1)
    slope = _pow2_neg(ALIBI_LOG2_STEP * ((r // rows_h) + 1))
    tri = _strict_lower_ones(page)

    def sample_stages(blocks, masked):
        qa = qa_ref[...]
        qb = qb_ref[...]
        scores = [_dot(qa, blk[0][...].astype(BF16)) for blk in blocks]
        logits = [_dot(qb, blk[2][...].astype(BF16)) for blk in blocks]
        yield

        for i, blk in enumerate(blocks):
            scores[i] = scores[i] + slope * (col + blk[4])
            if masked:
                scores[i] = jnp.where(c <= q_of_row, scores[i], NEG_INF)
        m = m_sc[...]
        m_new = jnp.maximum(m, jnp.max(functools.reduce(jnp.maximum, scores), axis=-1, keepdims=True))
        alpha = jnp.exp(m - m_new)
        probs = [jnp.exp(s - m_new) for s in scores]
        m_sc[...] = m_new
        l_sc[...] = alpha * l_sc[...] + jnp.sum(functools.reduce(jnp.add, probs), axis=-1,
                                                keepdims=True)
        probs = [p.astype(BF16) for p in probs]

        mask = (c < q_of_row) if masked else None
        log_beta, drops, sums = [], [], []
        for z in logits:
            drop = _softplus(z)
            log_beta.append(z - drop)
            if masked:
                drop = jnp.where(mask, drop, 0.0)
            drops.append(drop.astype(BF16))
            sums.append(jnp.sum(drop, axis=-1, keepdims=True))
        after = [_dot(d, tri) for d in drops]
        yield

        for h in range(H_A):
            rows = slice(h * rows_h, (h + 1) * rows_h)
            upd = None
            for blk, p in zip(blocks, probs):
                v_h = blk[1][pl.ds(h, page, stride=H_A), :].astype(BF16)
                d = _dot(p[rows], v_h)
                upd = d if upd is None else upd + d
            acca_sc[rows, :] = alpha[rows] * acca_sc[rows, :] + upd

        run = run_sc[...]
        weights = []
        for lb, aft, rs in zip(log_beta, after, sums):
            w = jnp.exp(lb - aft - run)
            if masked:
                w = jnp.where(mask, w, 0.0)
            weights.append(w.astype(BF16))
            run = run + rs
        run_sc[...] = run
        upd = None
        for blk, w in zip(blocks, weights):
            d = _nt_dot(w, blk[3][...].astype(BF16))
            upd = d if upd is None else upd + d
        accb_sc[...] = accb_sc[...] + upd

    def page_stages():
        newest = n_pages - 1 - (j - 1) * PAGES_PER_STEP
        return sample_stages(
            [(pools[0][i], pools[1][i], pools[2][i], pools[3][i],
              ((newest - i) * page - past).astype(F32)) for i in range(PAGES_PER_STEP)], False)

    def prompt_stages(diag):
        return _diff_block_stages(q_ref, k_ref, v_ref, dm_sc, dacc_sc, head0, kb,
                                  ((kb - qi) * TQ).astype(F32), diag)

    is_diag = kb == qi
    has_pages = jnp.logical_and(sample_live, j > 0)

    @pl.when(kb == 0)
    def _():
        dm_sc[...] = jnp.full(dm_sc.shape, NEG_INF, F32)
        dacc_sc[...] = jnp.zeros(dacc_sc.shape, F32)

    @pl.when(jnp.logical_and(sample_live, j == 0))
    def _():
        m_sc[...] = jnp.full(m_sc.shape, NEG_INF, F32)
        l_sc[...] = jnp.zeros(l_sc.shape, F32)
        acca_sc[...] = jnp.zeros(acca_sc.shape, F32)
        run_sc[...] = jnp.zeros(run_sc.shape, F32)
        accb_sc[...] = jnp.zeros(accb_sc.shape, F32)
        _run_interleaved(sample_stages([(kna_ref, vna_ref, knb_ref, vnb_ref, 0.0)], True))

    for diag in (False, True):
        on_diag = is_diag if diag else jnp.logical_not(is_diag)

        @pl.when(jnp.logical_and(on_diag, has_pages))
        def _(diag=diag):
            _run_interleaved(prompt_stages(diag), page_stages())
            if diag:
                _diff_finalize(dacc_sc, lam_ref, g_ref, o_ref, lam_init)

        @pl.when(jnp.logical_and(on_diag, jnp.logical_not(has_pages)))
        def _(diag=diag):
            _run_interleaved(prompt_stages(diag))
            if diag:
                _diff_finalize(dacc_sc, lam_ref, g_ref, o_ref, lam_init)

    @pl.when(jnp.logical_and(sample_live, j == steps_per_row - 1))
    def _():
        o = acca_sc[...] / l_sc[...]
        lam = _lambda(lam_ref, lam_init)
        for h in range(H_A):
            d = (o[h * rows_h:h * rows_h + Q_PAD]
                 - lam * o[h * rows_h + Q_PAD:(h + 1) * rows_h])
            oa_ref[:, h * DV_A:(h + 1) * DV_A] = _rms(d, g_ref[...]) * (1.0 - lam_init)
        acc = accb_sc[...]
        lane_head = lax.broadcasted_iota(jnp.int32, (Q_PAD, W_GROUP), 1) // DH_B
        out = jnp.zeros((Q_PAD, W_GROUP), F32)
        for h in range(H_B):
            out = out + jnp.where(lane_head == h, acc[h * Q_PAD:(h + 1) * Q_PAD], 0.0)
        ob_ref[...] = out


def _diff_and_sample_attention(zb, page_table, qbd_a, qbd_b, kn_a, vn_a, kn_b, vn_b,
                               pool_arrays, lam_pack, g_subln, pool_base, lam_init):
    b, t, _ = zb.shape
    db, n_pages = page_table.shape
    page = pool_arrays[0].shape[2]
    past = n_pages * page
    assert n_pages % PAGES_PER_STEP == 0 and t % TQ == 0
    width = DIFF_STREAMS * LANES
    n_groups = W_GROUP // width
    nq = t // TQ
    pairs = [(qi, kb) for qi in range(nq) for kb in range(qi + 1)]
    qtab = jnp.array([p[0] for p in pairs], jnp.int32)
    ktab = jnp.array([p[1] for p in pairs], jnp.int32)
    bpg = len(pairs)
    n_steps = b * n_groups * bpg
    steps_per_row = n_pages // PAGES_PER_STEP + 1
    sample_steps = db * steps_per_row
    assert sample_steps <= n_steps, "the prompt sweep must have a step for every page group"

    def prompt_map(col_group):
        def index(s, pt, qt, kt):
            bg = s // bpg
            return (bg // n_groups, 0, col_group * n_groups + bg % n_groups)
        return index

    def q_index(s, pt, qt, kt):
        bg = s // bpg
        return (bg // n_groups, qt[s % bpg], bg % n_groups)

    def row_index(s, pt, qt, kt):
        return (jnp.minimum(s, sample_steps - 1) // steps_per_row, 0, 0)

    def paged(i):
        def index(s, pt, qt, kt):
            s = jnp.minimum(s, sample_steps - 1)
            j = s % steps_per_row
            newest = n_pages - 1 - (jnp.maximum(j, 1) - 1) * PAGES_PER_STEP
            return (pool_base + pt[s // steps_per_row, newest - i], 0, 0)
        return pl.BlockSpec((None, W_GROUP, page), index)

    fixed = lambda nd: (lambda s, pt, qt, kt: (0,) * nd)
    new_spec = pl.BlockSpec((None, W_GROUP, page), row_index)
    q_spec = pl.BlockSpec((None, ROWS_S, W_GROUP), row_index)
    out_spec = pl.BlockSpec((None, Q_PAD, W_GROUP), row_index)
    out = jax.ShapeDtypeStruct((db, Q_PAD, W_GROUP), F32)
    cache_specs = [paged(i) for _ in pool_arrays for i in range(PAGES_PER_STEP)]
    cache_args = [a for a in pool_arrays for _ in range(PAGES_PER_STEP)]
    return pl.pallas_call(
        functools.partial(_diff_sample_kernel, past=past, page=page, lam_init=lam_init,
                          blocks_per_group=bpg, n_groups=n_groups, sample_steps=sample_steps),
        grid_spec=pltpu.PrefetchScalarGridSpec(
            num_scalar_prefetch=3,
            grid=(n_steps,),
            in_specs=[pl.BlockSpec((None, TQ, width), q_index),
                      pl.BlockSpec((None, t, width), prompt_map(1)),
                      pl.BlockSpec((None, t, width), prompt_map(2)),
                      pl.BlockSpec(lam_pack.shape, fixed(3)), pl.BlockSpec(g_subln.shape, fixed(2)),
                      q_spec, q_spec, new_spec, new_spec, new_spec, new_spec] + cache_specs,
            out_specs=[pl.BlockSpec((None, TQ, width), q_index), out_spec, out_spec],
            scratch_shapes=[pltpu.VMEM((DIFF_STREAMS, 2 * TQ, LANES), F32),
                            pltpu.VMEM((DIFF_STREAMS, 2 * TQ, 2 * LANES), F32),
                            pltpu.VMEM((ROWS_S, 1), F32), pltpu.VMEM((ROWS_S, 1), F32),
                            pltpu.VMEM((ROWS_S, DV_A), F32), pltpu.VMEM((ROWS_S, 1), F32),
                            pltpu.VMEM((ROWS_S, W_GROUP), F32)]),
        out_shape=[jax.ShapeDtypeStruct((b, t, W_GROUP), F32), out, out],
        compiler_params=pltpu.CompilerParams(dimension_semantics=("arbitrary",),
                                             vmem_limit_bytes=VMEM_LIMIT),
        name="diff_sample_attention",
    )(page_table, qtab, ktab, zb, zb, zb, lam_pack, g_subln,
      qbd_a, qbd_b, kn_a, vn_a, kn_b, vn_b, *cache_args)


def _sigmoid(x):
    return 1.0 / (1.0 + jnp.exp(-x))


def _merge_kernel(x_ref, oa_ref, ob_ref, gate_ref, p_ref, wo_ref, gp_ref, wg_ref, wp_ref, gf_ref,
                  y_ref, *, final):
    gate = gate_ref[...]
    sg = gate * _sigmoid(gate)
    ua = (oa_ref[...] * sg[:, :W_GROUP]).astype(BF16)
    ub = (ob_ref[...] * sg[:, W_GROUP:]).astype(BF16)
    x1 = x_ref[...] + _dot(ua, wo_ref[:W_GROUP, :]) + _dot(ub, wo_ref[W_GROUP:, :])
    hn = _rms(x1, gp_ref[...]).astype(BF16)
    g = _sigmoid(_dot(hn, wg_ref[...]))
    pe = _dot(p_ref[...].astype(BF16), wp_ref[...])
    x2 = x1 + g * pe
    y_ref[...] = _rms(x2, gf_ref[...]) if final else x2


def _merge(x, o_a, o_b, gate, p, w_out, g_ple, w_gate, w_proj, g_final, tm, final):
    n, d = x.shape
    assert n % tm == 0
    row = lambda i: (i, 0)
    fixed = lambda i: (0, 0)
    full = lambda a: pl.BlockSpec(a.shape, fixed)
    g_ple = g_ple.reshape(1, d)
    g_final = g_final.reshape(1, d)
    return pl.pallas_call(
        functools.partial(_merge_kernel, final=final),
        grid=(n // tm,),
        in_specs=[pl.BlockSpec((tm, d), row), pl.BlockSpec((tm, W_GROUP), row),
                  pl.BlockSpec((tm, W_GROUP), row), pl.BlockSpec((tm, MIX), row),
                  pl.BlockSpec((tm, p.shape[1]), row), full(w_out), full(g_ple), full(w_gate),
                  full(w_proj), full(g_final)],
        out_specs=pl.BlockSpec((tm, d), row),
        out_shape=jax.ShapeDtypeStruct((n, d), F32),
        compiler_params=pltpu.CompilerParams(dimension_semantics=("parallel",),
                                             vmem_limit_bytes=VMEM_LIMIT),
        name="merge",
    )(x, o_a, o_b, gate, p, w_out, g_ple, w_gate, w_proj, g_final)


def _block_diag_queries(q, group_width):
    n_groups = W_GROUP // group_width
    lane_group = jnp.arange(W_GROUP, dtype=jnp.int32) // group_width
    mask = (lane_group[None, :] == jnp.arange(n_groups, dtype=jnp.int32)[:, None]).astype(q.dtype)
    out = q[:, None, :, :] * mask[None, :, None, :]
    return out.reshape(q.shape[0], n_groups * Q_PAD, W_GROUP)


def _tokens_major(kt, lead, tail):
    nb, _, rows = kt.shape
    a = jnp.moveaxis(kt.reshape(nb, *tail, rows), -1, 1)
    return a.reshape(*lead, *tail)


def kernel(x_prompt, x_sample, cache_k_diff, cache_v_diff, cache_k_sb, cache_v_sb, page_table,
           p_prompt, p_sample, g_norm, w_in, lambda_q1, lambda_k1, lambda_q2, lambda_k2,
           g_subln, w_out, g_ple, w_ple_gate, w_ple_proj, g_final):
    b, t, d = x_prompt.shape
    db, ds, _ = x_sample.shape
    depth, n_pool, page = cache_k_diff.shape[:3]
    assert ds <= Q_PAD and page * H_A == W_GROUP and DV_A == page

    xp = x_prompt.reshape(b * t, d)
    xs = x_sample.reshape(db * ds, d)
    n_all = depth * n_pool
    pools = [cache_k_diff.transpose(0, 1, 3, 4, 5, 2).reshape(n_all, W_GROUP, page),
             cache_v_diff.reshape(n_all, page * H_A, DV_A),
             cache_k_sb.transpose(0, 1, 3, 4, 2).reshape(n_all, W_GROUP, page),
             cache_v_sb.transpose(0, 1, 3, 4, 2).reshape(n_all, W_GROUP, page)]

    kd, vd, ksb = (H_A, 2, DH_A), (H_A, DV_A), (H_B, DH_B)
    new_p = [[] for _ in range(4)]
    new_s = [[] for _ in range(4)]
    for l in range(depth):
        lam_init = 0.8 - 0.6 * math.exp(-0.3 * l)
        final = l == depth - 1
        w_in_l = w_in[l].astype(BF16)
        w_out_l = w_out[l].astype(BF16)
        w_gate_l = w_ple_gate[l].astype(BF16)
        w_proj_l = w_ple_proj[l].astype(BF16)
        lam_pack = jnp.stack([lambda_q1[l], lambda_k1[l], lambda_q2[l], lambda_k2[l]])[:, None, :]
        g_sub = g_subln[l].reshape(1, DV_A)

        zb, kta, va, ktb, vtb, gate = _proj(xp, g_norm[l], w_in_l, 512, t)
        zb3 = zb.reshape(b, t, N_GROUPS_QKV * W_GROUP)
        new_p[0].append(_tokens_major(kta, (b, t), kd))
        new_p[1].append(va.reshape(b, t, *vd))
        new_p[2].append(_tokens_major(ktb, (b, t), ksb))
        new_p[3].append(_tokens_major(vtb, (b, t), ksb))
        gate_p = gate
        zb, kta, va, ktb, vtb, gate = _proj(xs, g_norm[l], w_in_l, db * ds, db * ds)
        zs = zb.reshape(db, ds, N_GROUPS_QKV, W_GROUP)
        q_pad = lambda i: jnp.pad(zs[:, :, i], ((0, 0), (0, Q_PAD - ds), (0, 0)))
        per_b = lambda kt: jnp.pad(kt.reshape(W_GROUP, db, ds).transpose(1, 0, 2),
                                   ((0, 0), (0, 0), (0, page - ds)))
        vn_a = jnp.pad(va.reshape(db, ds, H_A, DV_A), ((0, 0), (0, page - ds), (0, 0), (0, 0)))

        o_a, so_a, so_b = _diff_and_sample_attention(
            zb3, page_table, _block_diag_queries(q_pad(0), DH_A),
            _block_diag_queries(q_pad(3), DH_B),
            per_b(kta), vn_a.reshape(db, page * H_A, DV_A), per_b(ktb), per_b(vtb),
            pools, lam_pack, g_sub, l * n_pool, lam_init)
        o_b = _prompt_attention(_sb_prompt_kernel, zb3, [], 3, 4, 5,
                                [pltpu.VMEM((STREAMS, 2 * TQ, LANES), F32),
                                 pltpu.VMEM((STREAMS, TQ, LANES), F32)], "sb_prompt")

        xp = _merge(xp, o_a.reshape(b * t, W_GROUP), o_b.reshape(b * t, W_GROUP), gate_p,
                    p_prompt[l].reshape(b * t, -1), w_out_l, g_ple[l], w_gate_l, w_proj_l,
                    g_final, 512, final)
        xs = _merge(xs, so_a[:, :ds].reshape(db * ds, W_GROUP), so_b[:, :ds].reshape(db * ds, W_GROUP),
                    gate, p_sample[l].reshape(db * ds, -1), w_out_l, g_ple[l], w_gate_l, w_proj_l,
                    g_final, db * ds, final)
        new_s[0].append(_tokens_major(kta, (db, ds), kd))
        new_s[1].append(va.reshape(db, ds, *vd))
        new_s[2].append(_tokens_major(ktb, (db, ds), ksb))
        new_s[3].append(_tokens_major(vtb, (db, ds), ksb))

    stacked = lambda parts: jnp.stack(parts) if depth > 1 else parts[0][None]
    return (xp.reshape(b, t, d), xs.reshape(db, ds, d),
            *[stacked(parts) for parts in new_p], *[stacked(parts) for parts in new_s])
```

```python
import functools
import math

import jax
import jax.numpy as jnp
from jax import lax
from jax.experimental import pallas as pl
from jax.experimental.pallas import tpu as pltpu

F32 = jnp.float32
BF16 = jnp.bfloat16

H_A = 4
DH_A = 64
DV_A = 2 * DH_A
H_B = 8
DH_B = 64
EPS = 1e-6
NEG_INF = -1e30
LOG2_E = math.log2(math.e)
W_GROUP = H_A * DV_A
N_GROUPS_QKV = 6
TOKEN_MINOR_GROUPS = (1, 4, 5)
MIX = 2 * W_GROUP

LANES = 128
SUBLANES = 8
HALF = LANES // 2
VMEM_LIMIT = 56 * 1024 * 1024

ALIBI_LOG2_STEP = 8 // H_A
assert 8 % H_A == 0

TQ = 512
TK_SB = 256
ROW_CHUNK = 128
STREAMS = 4
DIFF_STREAMS = 2
Q_PAD = SUBLANES
ROWS_S = 2 * H_A * Q_PAD
assert ROWS_S == H_B * Q_PAD
PAGES_PER_STEP = 8


def _pow2_neg(e):
    return lax.bitcast_convert_type((127 - e) << 23, F32)


def _rms(x, g):
    return x * lax.rsqrt(jnp.mean(x * x, axis=-1, keepdims=True) + EPS) * g


def _nt_dot(a, b):
    return lax.dot_general(a, b, (((1,), (1,)), ((), ())), preferred_element_type=F32)


def _dot(a, b):
    return jnp.dot(a, b, preferred_element_type=F32)


def _lambda(lam_ref, lam_init):
    t1 = jnp.sum(lam_ref[0] * lam_ref[1], axis=-1, keepdims=True)
    t2 = jnp.sum(lam_ref[2] * lam_ref[3], axis=-1, keepdims=True)
    return jnp.exp(t1) - jnp.exp(t2) + lam_init


def _softplus(z):
    return jnp.maximum(z, 0.0) + jnp.log(1.0 + jnp.exp2(jnp.abs(z) * (-LOG2_E)))


def _strict_lower_ones(n):
    r = lax.broadcasted_iota(jnp.int32, (n, n), 0)
    c = lax.broadcasted_iota(jnp.int32, (n, n), 1)
    return jnp.where(r > c, 1.0, 0.0).astype(BF16)


def _proj_kernel(x_ref, g_ref, w_ref, zb_ref, kta_ref, va_ref, ktb_ref, vtb_ref, gate_ref):
    h = _rms(x_ref[...], g_ref[...]).astype(BF16)
    tm = h.shape[0]

    def cols(i):
        return _dot(h, w_ref[:, i * W_GROUP:(i + 1) * W_GROUP])

    token_minor = dict(zip(TOKEN_MINOR_GROUPS, (kta_ref, ktb_ref, vtb_ref)))
    for i in range(N_GROUPS_QKV):
        z = cols(i)
        if i == 2:
            for hh in range(H_A):
                va_ref[pl.ds(hh, tm, stride=H_A), :] = z[:, hh * DV_A:(hh + 1) * DV_A]
        if i in token_minor:
            token_minor[i][...] = z.T
        if i in (0, 3):
            z = z * (DH_A ** -0.5)
        zb_ref[:, i * W_GROUP:(i + 1) * W_GROUP] = z.astype(BF16)
    for i in range(2):
        gate_ref[:, i * W_GROUP:(i + 1) * W_GROUP] = cols(N_GROUPS_QKV + i)


def _proj(x, g, w_bf16, tm, rows_per_batch):
    n, d = x.shape
    d_in = w_bf16.shape[1]
    assert n % rows_per_batch == 0 and rows_per_batch % tm == 0
    assert d_in == (N_GROUPS_QKV + 2) * W_GROUP
    tiles_per_batch = rows_per_batch // tm
    row = lambda i: (i, 0)
    fixed = lambda i: (0, 0)
    tok_minor = lambda i: (i // tiles_per_batch, 0, i % tiles_per_batch)
    kt = jax.ShapeDtypeStruct((n // rows_per_batch, W_GROUP, rows_per_batch), F32)
    kt_spec = pl.BlockSpec((None, W_GROUP, tm), tok_minor)
    return pl.pallas_call(
        _proj_kernel,
        grid=(n // tm,),
        in_specs=[pl.BlockSpec((tm, d), row), pl.BlockSpec((1, d), fixed),
                  pl.BlockSpec((d, d_in), fixed)],
        out_specs=[pl.BlockSpec((tm, N_GROUPS_QKV * W_GROUP), row), kt_spec,
                   pl.BlockSpec((tm * H_A, DV_A), row), kt_spec, kt_spec,
                   pl.BlockSpec((tm, MIX), row)],
        out_shape=[jax.ShapeDtypeStruct((n, N_GROUPS_QKV * W_GROUP), BF16), kt,
                   jax.ShapeDtypeStruct((n * H_A, DV_A), F32), kt, kt,
                   jax.ShapeDtypeStruct((n, MIX), F32)],
        compiler_params=pltpu.CompilerParams(dimension_semantics=("parallel",),
                                             vmem_limit_bytes=VMEM_LIMIT),
        name="proj",
    )(x, g.reshape(1, d), w_bf16)


def _stack_halves(x):
    lane = lax.broadcasted_iota(jnp.int32, x.shape, 1)
    zero = jnp.zeros_like(x)
    return jnp.concatenate([jnp.where(lane < HALF, x, zero), jnp.where(lane >= HALF, x, zero)], axis=0)


def _row_chunks():
    return [(r0, r0 + ROW_CHUNK) for r0 in range(0, 2 * TQ, ROW_CHUNK)]


def _causal_iotas(r0, tk):
    r = (lax.broadcasted_iota(jnp.int32, (ROW_CHUNK, tk), 0) + r0) & (TQ - 1)
    c = lax.broadcasted_iota(jnp.int32, (ROW_CHUNK, tk), 1)
    return r, c


def _lane_replicated(x):
    return jnp.broadcast_to(x, (x.shape[0], LANES))


def _lane_tiled(x, width):
    return jnp.concatenate([x] * (width // LANES), axis=1)


def _stream_lanes():
    return [slice(s * LANES, (s + 1) * LANES) for s in range(STREAMS)]


def _run_interleaved(*stage_generators):
    live = list(stage_generators)
    while live:
        for gen in list(live):
            try:
                next(gen)
            except StopIteration:
                live.remove(gen)


def _diff_block_stages(q_ref, k_ref, v_ref, m_sc, acc_sc, head0, kj, key_offset, diag):
    tk = TQ
    lanes = [slice(s * LANES, (s + 1) * LANES) for s in range(DIFF_STREAMS)]
    chunks = _row_chunks()
    qs = [_stack_halves(q_ref[:, ln]) for ln in lanes]
    slope = [_pow2_neg(jnp.full((1, tk), ALIBI_LOG2_STEP, jnp.int32) * (head0 + s + 1))
             for s in range(DIFF_STREAMS)]
    col = lax.broadcasted_iota(jnp.int32, (1, tk), 1).astype(F32)
    ones = jnp.ones((tk, LANES), BF16)
    start = pl.multiple_of(kj * tk, tk)
    s_all = [_nt_dot(qs[s], k_ref[pl.ds(start, tk), ln]) for s, ln in enumerate(lanes)]
    vs = [jnp.concatenate([v_ref[pl.ds(start, tk), ln], ones], axis=1) for ln in lanes]
    yield
    bias = [sl * (col + key_offset) for sl in slope]

    def scores(s, r0, r1):
        x = s_all[s][r0:r1] + bias[s]
        if diag:
            r, c = _causal_iotas(r0, tk)
            x = jnp.where(c <= r, x, NEG_INF)
        return x

    m_new = [jnp.concatenate(
        [jnp.maximum(m_sc[s, r0:r1, :],
                     _lane_replicated(jnp.max(scores(s, r0, r1), axis=-1, keepdims=True)))
         for r0, r1 in chunks], axis=0) for s in range(DIFF_STREAMS)]
    p = [jnp.concatenate(
        [jnp.exp(scores(s, r0, r1) - _lane_tiled(m_new[s][r0:r1], tk)).astype(BF16)
         for r0, r1 in chunks], axis=0) for s in range(DIFF_STREAMS)]
    yield
    for s in range(DIFF_STREAMS):
        alpha = jnp.exp(m_sc[s] - m_new[s])
        acc_sc[s] = _lane_tiled(alpha, 2 * LANES) * acc_sc[s] + _dot(p[s], vs[s])
        m_sc[s] = m_new[s]


def _diff_finalize(acc_sc, lam_ref, g_ref, o_ref, lam_init):
    lam = _lambda(lam_ref, lam_init)
    for s in range(DIFF_STREAMS):
        acc = acc_sc[s]
        o = acc[:, :LANES] / acc[:, LANES:]
        d = o[:TQ] - lam * o[TQ:]
        o_ref[:, s * LANES:(s + 1) * LANES] = _rms(d, g_ref[...]) * (1.0 - lam_init)


def _sb_prompt_kernel(q_ref, k_ref, v_ref, o_ref, run_sc, acc_sc):
    tk = TK_SB
    n_sub = TQ // tk
    qi = pl.program_id(2)
    lanes = _stream_lanes()
    chunks = _row_chunks()
    qs = [_stack_halves(q_ref[:, ln]) for ln in lanes]
    tri = _strict_lower_ones(tk)
    run_sc[...] = jnp.zeros(run_sc.shape, F32)
    acc_sc[...] = jnp.zeros(acc_sc.shape, F32)

    def block(kj, diag_sub=None):
        start = pl.multiple_of(kj * tk, tk)
        vs = [_stack_halves(v_ref[pl.ds(start, tk), ln]) for ln in lanes]
        row0 = 0 if diag_sub is None else diag_sub * tk
        live = [c for c in chunks if (c[0] & (TQ - 1)) >= row0]
        half = len(live) // 2
        q_live = qs if row0 == 0 else [
            jnp.concatenate([q[row0:TQ], q[TQ + row0:]], axis=0) for q in qs]
        z_all = [_nt_dot(q_live[s], k_ref[pl.ds(start, tk), ln]) for s, ln in enumerate(lanes)]
        masks = [None] * len(live)
        if diag_sub is not None:
            for i, (r0, _) in enumerate(live):
                r, c = _causal_iotas(r0, tk)
                masks[i] = c + diag_sub * tk < r
        log_beta, drops, sums = [], [], []
        for s in range(STREAMS):
            lb_s, drop_s, sum_s = [], [], []
            for i, mask in enumerate(masks):
                z = z_all[s][i * ROW_CHUNK:(i + 1) * ROW_CHUNK]
                drop = _softplus(z)
                lb_s.append(z - drop)
                if mask is not None:
                    drop = jnp.where(mask, drop, 0.0)
                drop_s.append(drop.astype(BF16))
                sum_s.append(_lane_replicated(jnp.sum(drop, axis=-1, keepdims=True)))
            log_beta.append(lb_s)
            drops.append(jnp.concatenate(drop_s, axis=0))
            sums.append(sum_s)
        after = [_dot(drops[s], tri) for s in range(STREAMS)]
        for s in range(STREAMS):
            w_out = []
            for i, ((r0, r1), mask, lb) in enumerate(zip(live, masks, log_beta[s])):
                run = run_sc[s, r0:r1, :]
                w = jnp.exp(lb - after[s][i * ROW_CHUNK:(i + 1) * ROW_CHUNK] - _lane_tiled(run, tk))
                if mask is not None:
                    w = jnp.where(mask, w, 0.0)
                w_out.append(w.astype(BF16))
                run_sc[s, r0:r1, :] = run + sums[s][i]
            wb = jnp.concatenate([jnp.concatenate(w_out[:half], axis=0),
                                  jnp.concatenate(w_out[half:], axis=0)], axis=1)
            acc_sc[s, row0:, :] = acc_sc[s, row0:, :] + _dot(wb, vs[s])

    def full_block(i, _):
        block(qi * n_sub - 1 - i)
        return 0

    for sub in reversed(range(n_sub)):
        block(qi * n_sub + sub, diag_sub=sub)
    lax.fori_loop(0, qi * n_sub, full_block, 0)
    for s, ln in enumerate(lanes):
        o_ref[:, ln] = acc_sc[s]


def _prompt_attention(body, zb, extra, q_blk, k_blk, v_blk, scratch, name):
    b, t, _ = zb.shape
    assert t % TQ == 0 and TQ % TK_SB == 0
    width = STREAMS * LANES
    assert W_GROUP % width == 0
    n_grp = W_GROUP // width
    extra_specs = [pl.BlockSpec(e.shape, lambda bi, g, qi, nd=e.ndim: (0,) * nd) for e in extra]
    return pl.pallas_call(
        body,
        grid=(b, n_grp, t // TQ),
        in_specs=[pl.BlockSpec((None, TQ, width), lambda bi, g, qi: (bi, qi, q_blk * n_grp + g)),
                  pl.BlockSpec((None, t, width), lambda bi, g, qi: (bi, 0, k_blk * n_grp + g)),
                  pl.BlockSpec((None, t, width), lambda bi, g, qi: (bi, 0, v_blk * n_grp + g))]
        + extra_specs,
        out_specs=pl.BlockSpec((None, TQ, width), lambda bi, g, qi: (bi, qi, g)),
        out_shape=jax.ShapeDtypeStruct((b, t, W_GROUP), F32),
        scratch_shapes=scratch,
        compiler_params=pltpu.CompilerParams(
            dimension_semantics=("parallel", "parallel", "parallel"),
            vmem_limit_bytes=VMEM_LIMIT),
        name=name,
    )(zb, zb, zb, *extra)


def _diff_sample_kernel(pt_ref, qtab_ref, ktab_ref, q_ref, k_ref, v_ref, lam_ref, g_ref,
                        qa_ref, qb_ref, kna_ref, vna_ref, knb_ref, vnb_ref, *rest,
                        past, page, lam_init, blocks_per_group, n_groups, sample_steps):
    del pt_ref
    n_cached = 4 * PAGES_PER_STEP
    pools = [rest[i * PAGES_PER_STEP:(i + 1) * PAGES_PER_STEP] for i in range(4)]
    (o_ref, oa_ref, ob_ref, dm_sc, dacc_sc,
     m_sc, l_sc, acca_sc, run_sc, accb_sc) = rest[n_cached:]
    t = pl.program_id(0)
    n_pages = past // page
    steps_per_row = n_pages // PAGES_PER_STEP + 1
    slot = t % blocks_per_group
    qi = qtab_ref[slot]
    kb = ktab_ref[slot]
    head0 = ((t // blocks_per_group) % n_groups) * DIFF_STREAMS
    j = t % steps_per_row
    sample_live = t < sample_steps
    rows_h = 2 * Q_PAD
    r = lax.broadcasted_iota(jnp.int32, (ROWS_S, page), 0)
    c = lax.broadcasted_iota(jnp.int32, (ROWS_S, page), 1)
    col = c.astype(F32)
    q_of_row = r & (Q_PAD - 1)
    slope = _pow2_neg(ALIBI_LOG2_STEP * ((r // rows_h) + 1))
    tri = _strict_lower_ones(page)

    def sample_stages(blocks, masked):
        qa = qa_ref[...]
        qb = qb_ref[...]
        scores = [_dot(qa, blk[0][...].astype(BF16)) for blk in blocks]
        logits = [_dot(qb, blk[2][...].astype(BF16)) for blk in blocks]
        yield

        for i, blk in enumerate(blocks):
            scores[i] = scores[i] + slope * (col + blk[4])
            if masked:
                scores[i] = jnp.where(c <= q_of_row, scores[i], NEG_INF)
        m = m_sc[...]
        m_new = jnp.maximum(m, jnp.max(functools.reduce(jnp.maximum, scores), axis=-1, keepdims=True))
        alpha = jnp.exp(m - m_new)
        probs = [jnp.exp(s - m_new) for s in scores]
        m_sc[...] = m_new
        l_sc[...] = alpha * l_sc[...] + jnp.sum(functools.reduce(jnp.add, probs), axis=-1,
                                                keepdims=True)
        probs = [p.astype(BF16) for p in probs]

        mask = (c < q_of_row) if masked else None
        log_beta, drops, sums = [], [], []
        for z in logits:
            drop = _softplus(z)
            log_beta.append(z - drop)
            if masked:
                drop = jnp.where(mask, drop, 0.0)
            drops.append(drop.astype(BF16))
            sums.append(jnp.sum(drop, axis=-1, keepdims=True))
        after = [_dot(d, tri) for d in drops]
        yield

        for h in range(H_A):
            rows = slice(h * rows_h, (h + 1) * rows_h)
            upd = None
            for blk, p in zip(blocks, probs):
                v_h = blk[1][pl.ds(h, page, stride=H_A), :].astype(BF16)
                d = _dot(p[rows], v_h)
                upd = d if upd is None else upd + d
            acca_sc[rows, :] = alpha[rows] * acca_sc[rows, :] + upd

        run = run_sc[...]
        weights = []
        for lb, aft, rs in zip(log_beta, after, sums):
            w = jnp.exp(lb - aft - run)
            if masked:
                w = jnp.where(mask, w, 0.0)
            weights.append(w.astype(BF16))
            run = run + rs
        run_sc[...] = run
        upd = None
        for blk, w in zip(blocks, weights):
            d = _nt_dot(w, blk[3][...].astype(BF16))
            upd = d if upd is None else upd + d
        accb_sc[...] = accb_sc[...] + upd

    def page_stages():
        newest = n_pages - 1 - (j - 1) * PAGES_PER_STEP
        return sample_stages(
            [(pools[0][i], pools[1][i], pools[2][i], pools[3][i],
              ((newest - i) * page - past).astype(F32)) for i in range(PAGES_PER_STEP)], False)

    def prompt_stages(diag):
        return _diff_block_stages(q_ref, k_ref, v_ref, dm_sc, dacc_sc, head0, kb,
                                  ((kb - qi) * TQ).astype(F32), diag)

    is_diag = kb == qi
    has_pages = jnp.logical_and(sample_live, j > 0)

    @pl.when(kb == 0)
    def _():
        dm_sc[...] = jnp.full(dm_sc.shape, NEG_INF, F32)
        dacc_sc[...] = jnp.zeros(dacc_sc.shape, F32)

    @pl.when(jnp.logical_and(sample_live, j == 0))
    def _():
        m_sc[...] = jnp.full(m_sc.shape, NEG_INF, F32)
        l_sc[...] = jnp.zeros(l_sc.shape, F32)
        acca_sc[...] = jnp.zeros(acca_sc.shape, F32)
        run_sc[...] = jnp.zeros(run_sc.shape, F32)
        accb_sc[...] = jnp.zeros(accb_sc.shape, F32)
        _run_interleaved(sample_stages([(kna_ref, vna_ref, knb_ref, vnb_ref, 0.0)], True))

    for diag in (False, True):
        on_diag = is_diag if diag else jnp.logical_not(is_diag)

        @pl.when(jnp.logical_and(on_diag, has_pages))
        def _(diag=diag):
            _run_interleaved(prompt_stages(diag), page_stages())
            if diag:
                _diff_finalize(dacc_sc, lam_ref, g_ref, o_ref, lam_init)

        @pl.when(jnp.logical_and(on_diag, jnp.logical_not(has_pages)))
        def _(diag=diag):
            _run_interleaved(prompt_stages(diag))
            if diag:
                _diff_finalize(dacc_sc, lam_ref, g_ref, o_ref, lam_init)

    @pl.when(jnp.logical_and(sample_live, j == steps_per_row - 1))
    def _():
        o = acca_sc[...] / l_sc[...]
        lam = _lambda(lam_ref, lam_init)
        for h in range(H_A):
            d = (o[h * rows_h:h * rows_h + Q_PAD]
                 - lam * o[h * rows_h + Q_PAD:(h + 1) * rows_h])
            oa_ref[:, h * DV_A:(h + 1) * DV_A] = _rms(d, g_ref[...]) * (1.0 - lam_init)
        acc = accb_sc[...]
        lane_head = lax.broadcasted_iota(jnp.int32, (Q_PAD, W_GROUP), 1) // DH_B
        out = jnp.zeros((Q_PAD, W_GROUP), F32)
        for h in range(H_B):
            out = out + jnp.where(lane_head == h, acc[h * Q_PAD:(h + 1) * Q_PAD], 0.0)
        ob_ref[...] = out


def _diff_and_sample_attention(zb, page_table, qbd_a, qbd_b, kn_a, vn_a, kn_b, vn_b,
                               pool_arrays, lam_pack, g_subln, pool_base, lam_init):
    b, t, _ = zb.shape
    db, n_pages = page_table.shape
    page = pool_arrays[0].shape[2]
    past = n_pages * page
    assert n_pages % PAGES_PER_STEP == 0 and t % TQ == 0
    width = DIFF_STREAMS * LANES
    n_groups = W_GROUP // width
    nq = t // TQ
    pairs = [(qi, kb) for qi in range(nq) for kb in range(qi + 1)]
    qtab = jnp.array([p[0] for p in pairs], jnp.int32)
    ktab = jnp.array([p[1] for p in pairs], jnp.int32)
    bpg = len(pairs)
    n_steps = b * n_groups * bpg
    steps_per_row = n_pages // PAGES_PER_STEP + 1
    sample_steps = db * steps_per_row
    assert sample_steps <= n_steps, "the prompt sweep must have a step for every page group"

    def prompt_map(col_group):
        def index(s, pt, qt, kt):
            bg = s // bpg
            return (bg // n_groups, 0, col_group * n_groups + bg % n_groups)
        return index

    def q_index(s, pt, qt, kt):
        bg = s // bpg
        return (bg // n_groups, qt[s % bpg], bg % n_groups)

    def row_index(s, pt, qt, kt):
        return (jnp.minimum(s, sample_steps - 1) // steps_per_row, 0, 0)

    def paged(i):
        def index(s, pt, qt, kt):
            s = jnp.minimum(s, sample_steps - 1)
            j = s % steps_per_row
            newest = n_pages - 1 - (jnp.maximum(j, 1) - 1) * PAGES_PER_STEP
            return (pool_base + pt[s // steps_per_row, newest - i], 0, 0)
        return pl.BlockSpec((None, W_GROUP, page), index)

    fixed = lambda nd: (lambda s, pt, qt, kt: (0,) * nd)
    new_spec = pl.BlockSpec((None, W_GROUP, page), row_index)
    q_spec = pl.BlockSpec((None, ROWS_S, W_GROUP), row_index)
    out_spec = pl.BlockSpec((None, Q_PAD, W_GROUP), row_index)
    out = jax.ShapeDtypeStruct((db, Q_PAD, W_GROUP), F32)
    cache_specs = [paged(i) for _ in pool_arrays for i in range(PAGES_PER_STEP)]
    cache_args = [a for a in pool_arrays for _ in range(PAGES_PER_STEP)]
    return pl.pallas_call(
        functools.partial(_diff_sample_kernel, past=past, page=page, lam_init=lam_init,
                          blocks_per_group=bpg, n_groups=n_groups, sample_steps=sample_steps),
        grid_spec=pltpu.PrefetchScalarGridSpec(
            num_scalar_prefetch=3,
            grid=(n_steps,),
            in_specs=[pl.BlockSpec((None, TQ, width), q_index),
                      pl.BlockSpec((None, t, width), prompt_map(1)),
                      pl.BlockSpec((None, t, width), prompt_map(2)),
                      pl.BlockSpec(lam_pack.shape, fixed(3)), pl.BlockSpec(g_subln.shape, fixed(2)),
                      q_spec, q_spec, new_spec, new_spec, new_spec, new_spec] + cache_specs,
            out_specs=[pl.BlockSpec((None, TQ, width), q_index), out_spec, out_spec],
            scratch_shapes=[pltpu.VMEM((DIFF_STREAMS, 2 * TQ, LANES), F32),
                            pltpu.VMEM((DIFF_STREAMS, 2 * TQ, 2 * LANES), F32),
                            pltpu.VMEM((ROWS_S, 1), F32), pltpu.VMEM((ROWS_S, 1), F32),
                            pltpu.VMEM((ROWS_S, DV_A), F32), pltpu.VMEM((ROWS_S, 1), F32),
                            pltpu.VMEM((ROWS_S, W_GROUP), F32)]),
        out_shape=[jax.ShapeDtypeStruct((b, t, W_GROUP), F32), out, out],
        compiler_params=pltpu.CompilerParams(dimension_semantics=("arbitrary",),
                                             vmem_limit_bytes=VMEM_LIMIT),
        name="diff_sample_attention",
    )(page_table, qtab, ktab, zb, zb, zb, lam_pack, g_subln,
      qbd_a, qbd_b, kn_a, vn_a, kn_b, vn_b, *cache_args)


def _sigmoid(x):
    return 1.0 / (1.0 + jnp.exp(-x))


def _merge_kernel(x_ref, oa_ref, ob_ref, gate_ref, p_ref, wo_ref, gp_ref, wg_ref, wp_ref, gf_ref,
                  y_ref, *, final):
    gate = gate_ref[...]
    sg = gate * _sigmoid(gate)
    ua = (oa_ref[...] * sg[:, :W_GROUP]).astype(BF16)
    ub = (ob_ref[...] * sg[:, W_GROUP:]).astype(BF16)
    x1 = x_ref[...] + _dot(ua, wo_ref[:W_GROUP, :]) + _dot(ub, wo_ref[W_GROUP:, :])
    hn = _rms(x1, gp_ref[...]).astype(BF16)
    g = _sigmoid(_dot(hn, wg_ref[...]))
    pe = _dot(p_ref[...].astype(BF16), wp_ref[...])
    x2 = x1 + g * pe
    y_ref[...] = _rms(x2, gf_ref[...]) if final else x2


def _merge(x, o_a, o_b, gate, p, w_out, g_ple, w_gate, w_proj, g_final, tm, final):
    n, d = x.shape
    assert n % tm == 0
    row = lambda i: (i, 0)
    fixed = lambda i: (0, 0)
    full = lambda a: pl.BlockSpec(a.shape, fixed)
    g_ple = g_ple.reshape(1, d)
    g_final = g_final.reshape(1, d)
    return pl.pallas_call(
        functools.partial(_merge_kernel, final=final),
        grid=(n // tm,),
        in_specs=[pl.BlockSpec((tm, d), row), pl.BlockSpec((tm, W_GROUP), row),
                  pl.BlockSpec((tm, W_GROUP), row), pl.BlockSpec((tm, MIX), row),
                  pl.BlockSpec((tm, p.shape[1]), row), full(w_out), full(g_ple), full(w_gate),
                  full(w_proj), full(g_final)],
        out_specs=pl.BlockSpec((tm, d), row),
        out_shape=jax.ShapeDtypeStruct((n, d), F32),
        compiler_params=pltpu.CompilerParams(dimension_semantics=("parallel",),
                                             vmem_limit_bytes=VMEM_LIMIT),
        name="merge",
    )(x, o_a, o_b, gate, p, w_out, g_ple, w_gate, w_proj, g_final)


def _block_diag_queries(q, group_width):
    n_groups = W_GROUP // group_width
    lane_group = jnp.arange(W_GROUP, dtype=jnp.int32) // group_width
    mask = (lane_group[None, :] == jnp.arange(n_groups, dtype=jnp.int32)[:, None]).astype(q.dtype)
    out = q[:, None, :, :] * mask[None, :, None, :]
    return out.reshape(q.shape[0], n_groups * Q_PAD, W_GROUP)


def _tokens_major(kt, lead, tail):
    nb, _, rows = kt.shape
    a = jnp.moveaxis(kt.reshape(nb, *tail, rows), -1, 1)
    return a.reshape(*lead, *tail)


def kernel(x_prompt, x_sample, cache_k_diff, cache_v_diff, cache_k_sb, cache_v_sb, page_table,
           p_prompt, p_sample, g_norm, w_in, lambda_q1, lambda_k1, lambda_q2, lambda_k2,
           g_subln, w_out, g_ple, w_ple_gate, w_ple_proj, g_final):
    b, t, d = x_prompt.shape
    db, ds, _ = x_sample.shape
    depth, n_pool, page = cache_k_diff.shape[:3]
    assert ds <= Q_PAD and page * H_A == W_GROUP and DV_A == page

    xp = x_prompt.reshape(b * t, d)
    xs = x_sample.reshape(db * ds, d)
    n_all = depth * n_pool
    pools = [cache_k_diff.transpose(0, 1, 3, 4, 5, 2).reshape(n_all, W_GROUP, page),
             cache_v_diff.reshape(n_all, page * H_A, DV_A),
             cache_k_sb.transpose(0, 1, 3, 4, 2).reshape(n_all, W_GROUP, page),
             cache_v_sb.transpose(0, 1, 3, 4, 2).reshape(n_all, W_GROUP, page)]

    kd, vd, ksb = (H_A, 2, DH_A), (H_A, DV_A), (H_B, DH_B)
    new_p = [[] for _ in range(4)]
    new_s = [[] for _ in range(4)]
    for l in range(depth):
        lam_init = 0.8 - 0.6 * math.exp(-0.3 * l)
        final = l == depth - 1
        w_in_l = w_in[l].astype(BF16)
        w_out_l = w_out[l].astype(BF16)
        w_gate_l = w_ple_gate[l].astype(BF16)
        w_proj_l = w_ple_proj[l].astype(BF16)
        lam_pack = jnp.stack([lambda_q1[l], lambda_k1[l], lambda_q2[l], lambda_k2[l]])[:, None, :]
        g_sub = g_subln[l].reshape(1, DV_A)

        zb, kta, va, ktb, vtb, gate = _proj(xp, g_norm[l], w_in_l, 512, t)
        zb3 = zb.reshape(b, t, N_GROUPS_QKV * W_GROUP)
        new_p[0].append(_tokens_major(kta, (b, t), kd))
        new_p[1].append(va.reshape(b, t, *vd))
        new_p[2].append(_tokens_major(ktb, (b, t), ksb))
        new_p[3].append(_tokens_major(vtb, (b, t), ksb))
        gate_p = gate
        zb, kta, va, ktb, vtb, gate = _proj(xs, g_norm[l], w_in_l, db * ds, db * ds)
        zs = zb.reshape(db, ds, N_GROUPS_QKV, W_GROUP)
        q_pad = lambda i: jnp.pad(zs[:, :, i], ((0, 0), (0, Q_PAD - ds), (0, 0)))
        per_b = lambda kt: jnp.pad(kt.reshape(W_GROUP, db, ds).transpose(1, 0, 2),
                                   ((0, 0), (0, 0), (0, page - ds)))
        vn_a = jnp.pad(va.reshape(db, ds, H_A, DV_A), ((0, 0), (0, page - ds), (0, 0), (0, 0)))

        o_a, so_a, so_b = _diff_and_sample_attention(
            zb3, page_table, _block_diag_queries(q_pad(0), DH_A),
            _block_diag_queries(q_pad(3), DH_B),
            per_b(kta), vn_a.reshape(db, page * H_A, DV_A), per_b(ktb), per_b(vtb),
            pools, lam_pack, g_sub, l * n_pool, lam_init)
        o_b = _prompt_attention(_sb_prompt_kernel, zb3, [], 3, 4, 5,
                                [pltpu.VMEM((STREAMS, 2 * TQ, LANES), F32),
                                 pltpu.VMEM((STREAMS, TQ, LANES), F32)], "sb_prompt")

        xp = _merge(xp, o_a.reshape(b * t, W_GROUP), o_b.reshape(b * t, W_GROUP), gate_p,
                    p_prompt[l].reshape(b * t, -1), w_out_l, g_ple[l], w_gate_l, w_proj_l,
                    g_final, 512, final)
        xs = _merge(xs, so_a[:, :ds].reshape(db * ds, W_GROUP), so_b[:, :ds].reshape(db * ds, W_GROUP),
                    gate, p_sample[l].reshape(db * ds, -1), w_out_l, g_ple[l], w_gate_l, w_proj_l,
                    g_final, db * ds, final)
        new_s[0].append(_tokens_major(kta, (db, ds), kd))
        new_s[1].append(va.reshape(db, ds, *vd))
        new_s[2].append(_tokens_major(ktb, (db, ds), ksb))
        new_s[3].append(_tokens_major(vtb, (db, ds), ksb))

    stacked = lambda parts: jnp.stack(parts) if depth > 1 else parts[0][None]
    return (xp.reshape(b, t, d), xs.reshape(db, ds, d),
            *[stacked(parts) for parts in new_p], *[stacked(parts) for parts in new_s])
```
